```python
import math
import jax, jax.numpy as jnp
from jax import lax
import numpy as np

D_MODEL = 1024
BATCH = 8
SEQ = 2048
DEPTH = 4

HEAD_DIM = 64
N_HEADS_TOTAL = D_MODEL // HEAD_DIM
N_HEADS_MOBA = N_HEADS_TOTAL // 2
N_HEADS_SWA = N_HEADS_TOTAL - N_HEADS_MOBA
N_KV_SWA = 2
GQA_GROUP = N_HEADS_SWA // N_KV_SWA
MIX_WIDTH = N_HEADS_TOTAL * HEAD_DIM
MOBA_BLOCK = 256
MOBA_TOPK = 3
MOBA_QCHUNK = 16
SWA_WINDOW = 128
N_BUCKETS = 32
MAX_DISTANCE = 128
D_FF = 4 * D_MODEL
EPS = 1e-6
W_A = N_HEADS_MOBA * HEAD_DIM
W_QB = N_HEADS_SWA * HEAD_DIM
W_KVB = N_KV_SWA * HEAD_DIM
D_IN = 3 * W_A + W_QB + 2 * W_KVB

kernel_name = "hybrid_moba_swa_sandwich_adaln"


def rmsnorm(x, g):
    xf = x.astype(jnp.float32)
    y = xf * lax.rsqrt(jnp.mean(xf * xf, axis=-1, keepdims=True) + EPS)
    return (y * g.astype(jnp.float32)).astype(x.dtype)


def t5_bucket(dist):
    n = jnp.maximum(dist, 0)
    max_exact = N_BUCKETS // 2
    nf = jnp.maximum(n, 1).astype(jnp.float32)
    large = max_exact + (jnp.log(nf / max_exact) / math.log(MAX_DISTANCE / max_exact)
                         * (N_BUCKETS - max_exact)).astype(jnp.int32)
    large = jnp.minimum(large, N_BUCKETS - 1)
    return jnp.where(n < max_exact, n, large)


def moba_attention(q, k, v, bias_hb):
    B, H, S, dh = q.shape
    nb = -(-S // MOBA_BLOCK)
    sp = nb * MOBA_BLOCK
    padw = ((0, 0), (0, 0), (0, sp - S), (0, 0))
    q, k, v = jnp.pad(q, padw), jnp.pad(k, padw), jnp.pad(v, padw)
    kb = k.reshape(B, H, nb, MOBA_BLOCK, dh)
    vb = v.reshape(B, H, nb, MOBA_BLOCK, dh)
    k_mean = jnp.mean(kb.astype(jnp.float32), axis=3)
    gate = jnp.einsum('bhsd,bhnd->bhsn', q.astype(jnp.float32), k_mean)
    q_block = jnp.arange(sp) // MOBA_BLOCK
    fully_past = jnp.arange(nb)[None, :] < q_block[:, None]
    gate = jnp.where(fully_past, gate, -jnp.inf)
    n_sel = min(MOBA_TOPK, nb)
    sel_score, sel_idx = lax.top_k(gate, n_sel)
    sel_valid = jnp.isfinite(sel_score)
    scale = dh ** -0.5
    b_idx = jnp.arange(B)[:, None, None, None]
    h_idx = jnp.arange(H)[None, :, None, None]
    offs = jnp.arange(MOBA_BLOCK)

    def chunk(ci):
        q0 = ci * MOBA_QCHUNK
        qc = lax.dynamic_slice_in_dim(q, q0, MOBA_QCHUNK, axis=2)
        idx = lax.dynamic_slice_in_dim(sel_idx, q0, MOBA_QCHUNK, axis=2)
        valid = lax.dynamic_slice_in_dim(sel_valid, q0, MOBA_QCHUNK, axis=2)
        qpos = q0 + jnp.arange(MOBA_QCHUNK)
        k_sel = kb[b_idx, h_idx, idx]
        v_sel = vb[b_idx, h_idx, idx]
        s_sel = jnp.einsum('bhqd,bhqnjd->bhqnj', qc, k_sel).astype(jnp.float32) * scale
        kpos_sel = idx[..., None] * MOBA_BLOCK + offs
        s_sel = s_sel + bias_hb[h_idx[..., None], t5_bucket(qpos[:, None, None] - kpos_sel)]
        s_sel = jnp.where(valid[..., None], s_sel, -jnp.inf)
        ob = q0 // MOBA_BLOCK
        k_own = lax.dynamic_slice_in_dim(k, ob * MOBA_BLOCK, MOBA_BLOCK, axis=2)
        v_own = lax.dynamic_slice_in_dim(v, ob * MOBA_BLOCK, MOBA_BLOCK, axis=2)
        s_own = jnp.einsum('bhqd,bhjd->bhqj', qc, k_own).astype(jnp.float32) * scale
        dist = qpos[:, None] - (ob * MOBA_BLOCK + offs)[None, :]
        s_own = s_own + bias_hb[:, t5_bucket(dist)][None]
        s_own = jnp.where(dist >= 0, s_own, -jnp.inf)
        logits = jnp.concatenate(
            [s_sel.reshape(B, H, MOBA_QCHUNK, n_sel * MOBA_BLOCK), s_own], axis=-1)
        p = jax.nn.softmax(logits, axis=-1)
        p_sel = p[..., :n_sel * MOBA_BLOCK].reshape(B, H, MOBA_QCHUNK, n_sel, MOBA_BLOCK).astype(v.dtype)
        p_own = p[..., n_sel * MOBA_BLOCK:].astype(v.dtype)
        return (jnp.einsum('bhqnj,bhqnjd->bhqd', p_sel, v_sel)
                + jnp.einsum('bhqj,bhjd->bhqd', p_own, v_own))

    outs = lax.map(chunk, jnp.arange(sp // MOBA_QCHUNK))
    out = jnp.moveaxis(outs, 0, 2).reshape(B, H, sp, dh)
    return out[:, :, :S]


def swa_attention(q, k, v, bias_hb, sinks):
    B, HKV, G, S, dh = q.shape
    W = SWA_WINDOW
    nbq = S // W
    qb = q.reshape(B, HKV, G, nbq, W, dh)
    kb = k.reshape(B, HKV, nbq, W, dh)
    vb = v.reshape(B, HKV, nbq, W, dh)
    pad_prev = ((0, 0), (0, 0), (1, 0), (0, 0), (0, 0))
    k_band = jnp.concatenate([jnp.pad(kb[:, :, :-1], pad_prev), kb], axis=3)
    v_band = jnp.concatenate([jnp.pad(vb[:, :, :-1], pad_prev), vb], axis=3)
    s = jnp.einsum('bkgnqd,bknjd->bkgnqj', qb, k_band).astype(jnp.float32) * (dh ** -0.5)
    blk = jnp.arange(nbq)
    qpos = blk[:, None] * W + jnp.arange(W)[None, :]
    kpos = (blk[:, None] - 1) * W + jnp.arange(2 * W)[None, :]
    dist = qpos[:, :, None] - kpos[:, None, :]
    allowed = (dist >= 0) & (dist < W) & (kpos[:, None, :] >= 0)
    bias = bias_hb[:, t5_bucket(dist)].reshape(HKV, G, nbq, W, 2 * W)
    s = jnp.where(allowed, s + bias, -jnp.inf)
    sink = jnp.broadcast_to(sinks.astype(jnp.float32).reshape(1, HKV, G, 1, 1, 1), s.shape[:-1] + (1,))
    p = jax.nn.softmax(jnp.concatenate([s, sink], axis=-1), axis=-1)[..., :-1]
    out = jnp.einsum('bkgnqj,bknjd->bkgnqd', p.astype(v.dtype), v_band)
    return out.reshape(B, HKV, G, S, dh)


def token_mixer(h, w_in, w_out, sinks, bias_hb):
    B, S, _ = h.shape
    proj = h @ w_in
    qa, ka, va, qb, kb, vb = jnp.split(
        proj, [W_A, 2 * W_A, 3 * W_A, 3 * W_A + W_QB, 3 * W_A + W_QB + W_KVB], axis=-1)
    heads = lambda t, n: t.reshape(B, S, n, HEAD_DIM).transpose(0, 2, 1, 3)
    out_a = moba_attention(heads(qa, N_HEADS_MOBA), heads(ka, N_HEADS_MOBA),
                           heads(va, N_HEADS_MOBA), bias_hb[:N_HEADS_MOBA])
    out_a = out_a.transpose(0, 2, 1, 3).reshape(B, S, W_A)
    q_swa = qb.reshape(B, S, N_KV_SWA, GQA_GROUP, HEAD_DIM).transpose(0, 2, 3, 1, 4)
    out_b = swa_attention(q_swa, heads(kb, N_KV_SWA), heads(vb, N_KV_SWA),
                          bias_hb[N_HEADS_MOBA:], sinks)
    out_b = out_b.transpose(0, 3, 1, 2, 4).reshape(B, S, W_QB)
    return jnp.concatenate([out_a, out_b], axis=-1) @ w_out


def setup_inputs(seed: int = 0) -> dict:
    key = jax.random.key(seed)
    ks = jax.random.split(key, 13)
    f32 = jnp.float32
    nrm = lambda k, shape, s: jax.random.normal(k, shape, f32) * s
    return {
        "x": nrm(ks[0], (BATCH, SEQ, D_MODEL), 1.0),
        "c": nrm(ks[1], (BATCH, D_MODEL), 1.0),
        "w_in": nrm(ks[2], (DEPTH, D_MODEL, D_IN), D_MODEL ** -0.5),
        "w_out": nrm(ks[3], (DEPTH, MIX_WIDTH, D_MODEL), MIX_WIDTH ** -0.5),
        "sinks": nrm(ks[4], (DEPTH, N_HEADS_SWA), 0.5),
        "rel_bias": nrm(ks[5], (N_BUCKETS, N_HEADS_TOTAL), 0.5),
        "w_ada": nrm(ks[6], (DEPTH, D_MODEL, 6 * D_MODEL), D_MODEL ** -0.5),
        "b_ada": nrm(ks[7], (DEPTH, 6 * D_MODEL), 0.02),
        "norm_gains": 1.0 + nrm(ks[8], (DEPTH, 4, D_MODEL), 0.05),
        "w1": nrm(ks[9], (DEPTH, D_MODEL, D_FF), D_MODEL ** -0.5),
        "b1": nrm(ks[10], (DEPTH, D_FF), 0.02),
        "w2": nrm(ks[11], (DEPTH, D_FF, D_MODEL), D_FF ** -0.5),
        "b2": nrm(ks[12], (DEPTH, D_MODEL), 0.02),
    }


def reference(x, c, w_in, w_out, sinks, rel_bias, w_ada, b_ada, norm_gains, w1, b1, w2, b2):
    bias_hb = rel_bias.T
    c_act = jax.nn.silu(c)
    for l in range(DEPTH):
        mod = c_act @ w_ada[l] + b_ada[l]
        sh1, sc1, g1, sh2, sc2, g2 = [m[:, None, :] for m in jnp.split(mod, 6, axis=-1)]
        h = rmsnorm(x, norm_gains[l, 0]) * (1.0 + sc1) + sh1
        y = token_mixer(h, w_in[l], w_out[l], sinks[l], bias_hb)
        x = x + g1 * rmsnorm(y, norm_gains[l, 1])
        h = rmsnorm(x, norm_gains[l, 2]) * (1.0 + sc2) + sh2
        y = jnp.square(jax.nn.relu(h @ w1[l] + b1[l])) @ w2[l] + b2[l]
        x = x + g2 * rmsnorm(y, norm_gains[l, 3])
    return x
```

```python
import functools
import math

import jax
import jax.numpy as jnp
from jax import lax
from jax.experimental import pallas as pl
from jax.experimental.pallas import tpu as pltpu

D_MODEL = 1024
HEAD_DIM = 64
N_HEADS_MOBA = 8
N_HEADS_SWA = 8
N_KV_SWA = 2
GQA_GROUP = N_HEADS_SWA // N_KV_SWA
MOBA_BLOCK = 256
MOBA_TOPK = 3
SWA_WINDOW = 128
N_BUCKETS = 32
MAX_DISTANCE = 128
D_FF = 4 * D_MODEL
EPS = 1e-6
W_A = N_HEADS_MOBA * HEAD_DIM
W_QB = N_HEADS_SWA * HEAD_DIM
W_KVB = N_KV_SWA * HEAD_DIM
D_IN = 3 * W_A + W_QB + 2 * W_KVB
SCALE = HEAD_DIM ** -0.5

LANES = 128
VMEM_LIMIT = 56 * 1024 * 1024
ROW_TILE = 512
FF_CHUNK = 1024
ADA_COL_TILE = 2048

F32 = jnp.float32
BF16 = jnp.bfloat16
NEG_INF = float("-inf")


def _rms(x, g):
    return x * lax.rsqrt(jnp.mean(x * x, axis=-1, keepdims=True) + EPS) * g


def _t5_bucket(dist):
    n = jnp.maximum(dist, 0)
    max_exact = N_BUCKETS // 2
    nf = jnp.maximum(n, 1).astype(F32)
    large = max_exact + (jnp.log(nf / max_exact) / math.log(MAX_DISTANCE / max_exact)
                         * (N_BUCKETS - max_exact)).astype(jnp.int32)
    large = jnp.minimum(large, N_BUCKETS - 1)
    return jnp.where(n < max_exact, n, large)


def _bias_from_dist(dist, rb_ref, head):
    bucket = _t5_bucket(dist)
    val = jnp.zeros(dist.shape, F32)
    for b in range(N_BUCKETS):
        val = jnp.where(bucket == b, rb_ref[b, head], val)
    return val


def _adaln_kernel(c_ref, w_ref, b_ref, o_ref):
    c = c_ref[...]
    c_act = c * (1.0 / (1.0 + jnp.exp(-c)))
    o_ref[...] = jnp.dot(c_act.astype(BF16), w_ref[...].astype(BF16),
                         preferred_element_type=F32) + b_ref[...]


def _adaln(c, w_ada, b_ada):
    depth, d, n = w_ada.shape
    bsz = c.shape[0]
    return pl.pallas_call(
        _adaln_kernel,
        grid=(depth, n // ADA_COL_TILE),
        in_specs=[
            pl.BlockSpec((bsz, d), lambda l, j: (0, 0)),
            pl.BlockSpec((None, d, ADA_COL_TILE), lambda l, j: (l, 0, j)),
            pl.BlockSpec((None, 1, ADA_COL_TILE), lambda l, j: (l, 0, j)),
        ],
        out_specs=pl.BlockSpec((None, bsz, ADA_COL_TILE), lambda l, j: (l, 0, j)),
        out_shape=jax.ShapeDtypeStruct((depth, bsz, n), F32),
        compiler_params=pltpu.CompilerParams(
            dimension_semantics=("arbitrary", "arbitrary"), vmem_limit_bytes=VMEM_LIMIT),
        name="adaln",
    )(c, w_ada, b_ada.reshape(depth, 1, n))


def _moba_bias_kernel(rb_ref, o_ref):
    h = pl.program_id(0)
    kk = lax.broadcasted_iota(jnp.int32, (MOBA_BLOCK, MOBA_BLOCK), 0)
    qq = lax.broadcasted_iota(jnp.int32, (MOBA_BLOCK, MOBA_BLOCK), 1)
    d_own = qq - kk
    o_ref[0] = jnp.where(d_own >= 0, _bias_from_dist(d_own, rb_ref, h), NEG_INF)
    o_ref[1] = _bias_from_dist(d_own + MOBA_BLOCK, rb_ref, h)


def _swa_bias_kernel(rb_ref, o_ref):
    h = pl.program_id(0) + N_HEADS_MOBA
    jj = lax.broadcasted_iota(jnp.int32, (2 * SWA_WINDOW, SWA_WINDOW), 0)
    qq = lax.broadcasted_iota(jnp.int32, (2 * SWA_WINDOW, SWA_WINDOW), 1)
    dist = qq + SWA_WINDOW - jj
    allowed = (dist >= 0) & (dist < SWA_WINDOW)
    o_ref[...] = jnp.where(allowed, _bias_from_dist(dist, rb_ref, h), NEG_INF)


def _bias_tiles(rel_bias):
    smem = pl.BlockSpec(memory_space=pltpu.SMEM)
    moba = pl.pallas_call(
        _moba_bias_kernel,
        grid=(N_HEADS_MOBA,),
        in_specs=[smem],
        out_specs=pl.BlockSpec((None, 2, MOBA_BLOCK, MOBA_BLOCK), lambda h: (h, 0, 0, 0)),
        out_shape=jax.ShapeDtypeStruct((N_HEADS_MOBA, 2, MOBA_BLOCK, MOBA_BLOCK), F32),
        compiler_params=pltpu.CompilerParams(dimension_semantics=("arbitrary",)),
        name="moba_bias",
    )(rel_bias)
    swa = pl.pallas_call(
        _swa_bias_kernel,
        grid=(N_HEADS_SWA,),
        in_specs=[smem],
        out_specs=pl.BlockSpec((None, 2 * SWA_WINDOW, SWA_WINDOW), lambda h: (h, 0, 0)),
        out_shape=jax.ShapeDtypeStruct((N_HEADS_SWA, 2 * SWA_WINDOW, SWA_WINDOW), F32),
        compiler_params=pltpu.CompilerParams(dimension_semantics=("arbitrary",)),
        name="swa_bias",
    )(rel_bias)
    return moba, swa


def _inproj_kernel(x_ref, mod_ref, g_ref, w_ref, proj_ref, kmean_ref):
    t = pl.program_id(1)
    h = _rms(x_ref[...], g_ref[0:1, :]) * (1.0 + mod_ref[1:2, :]) + mod_ref[0:1, :]
    hb = h.astype(BF16)
    blocks_per_tile = ROW_TILE // MOBA_BLOCK
    for c0 in range(0, D_IN, W_A):
        c1 = min(c0 + W_A, D_IN)
        p = jnp.dot(hb, w_ref[:, c0:c1], preferred_element_type=F32)
        proj_ref[:, c0:c1] = p.astype(BF16)
        if c0 == W_A:
            for r in range(blocks_per_tile):
                kmean_ref[pl.ds(t * blocks_per_tile + r, 1), :] = jnp.mean(
                    p[r * MOBA_BLOCK:(r + 1) * MOBA_BLOCK], axis=0, keepdims=True)


def _inproj(x, mod_l, gains_l, w_in_l):
    bsz, seq, d = x.shape
    nb = seq // MOBA_BLOCK
    return pl.pallas_call(
        _inproj_kernel,
        grid=(bsz, seq // ROW_TILE),
        in_specs=[
            pl.BlockSpec((None, ROW_TILE, d), lambda b, t: (b, t, 0)),
            pl.BlockSpec((None, 6, d), lambda b, t: (b, 0, 0)),
            pl.BlockSpec((4, d), lambda b, t: (0, 0)),
            pl.BlockSpec((d, D_IN), lambda b, t: (0, 0)),
        ],
        out_specs=[
            pl.BlockSpec((None, ROW_TILE, D_IN), lambda b, t: (b, t, 0)),
            pl.BlockSpec((None, nb, W_A), lambda b, t: (b, 0, 0)),
        ],
        out_shape=[
            jax.ShapeDtypeStruct((bsz, seq, D_IN), BF16),
            jax.ShapeDtypeStruct((bsz, nb, W_A), F32),
        ],
        compiler_params=pltpu.CompilerParams(
            dimension_semantics=("arbitrary", "arbitrary"), vmem_limit_bytes=VMEM_LIMIT),
        name="inproj",
    )(x, mod_l, gains_l, w_in_l)


def _online_update(s, v_t, m, l, acc):
    m_new = jnp.maximum(m, jnp.max(s, axis=0, keepdims=True))
    alpha = jnp.exp(m - m_new)
    p = jnp.exp(s - m_new)
    l = alpha * l + jnp.sum(p, axis=0, keepdims=True)
    acc = alpha * acc + jnp.dot(v_t, p.astype(BF16), preferred_element_type=F32)
    return m_new, l, acc


def _moba_kernel(cfar_ref, q_ref, k_ref, v_ref, km_ref, bias_ref, o_ref,
                 qt_scr, kh_scr, vt_scr, am_scr, *, nb):
    hp = pl.program_id(1)
    seq = nb * MOBA_BLOCK
    lane = lax.broadcasted_iota(jnp.int32, (1, LANES), 1)
    first_head = lane < HEAD_DIM
    row = lax.broadcasted_iota(jnp.int32, (LANES, 1), 0)
    first_rows = row < HEAD_DIM

    for i in range(nb):
        rows = slice(i * MOBA_BLOCK, (i + 1) * MOBA_BLOCK)
        qt_scr[i] = (q_ref[rows, :].astype(F32) * SCALE).T.astype(BF16)
        k_i = k_ref[rows, :].astype(F32)
        kh_scr[0, i] = jnp.where(first_head, k_i, 0.0).astype(BF16)
        kh_scr[1, i] = jnp.where(first_head, 0.0, k_i).astype(BF16)
        v_t = v_ref[rows, :].astype(F32).T
        vt_scr[0, i] = jnp.where(first_rows, v_t, 0.0).astype(BF16)
        vt_scr[1, i] = jnp.where(first_rows, 0.0, v_t).astype(BF16)

    j_io = lax.broadcasted_iota(jnp.int32, (nb, seq), 0)
    qblk = lax.broadcasted_iota(jnp.int32, (nb, seq), 1) // MOBA_BLOCK
    valid = j_io < qblk
    is_prev = j_io == qblk - 1
    q_all = q_ref[...]
    km = km_ref[...]
    for h in range(2):
        kmh = jnp.where(first_head, km, 0.0) if h == 0 else jnp.where(first_head, 0.0, km)
        p0 = kmh.astype(BF16)
        r1 = kmh - p0.astype(F32)
        p1 = r1.astype(BF16)
        p2 = (r1 - p1.astype(F32)).astype(BF16)
        lhs = jnp.concatenate([p0, p1, p2, jnp.zeros_like(p0)], axis=0)
        g4 = lax.dot_general(lhs, q_all, (((1,), (1,)), ((), ())), preferred_element_type=F32)
        gate = g4[0:nb] + g4[nb:2 * nb] + g4[2 * nb:3 * nb]
        gm = jnp.where(valid, gate, NEG_INF)
        cnt = jnp.zeros((nb, seq), jnp.int32)
        for jp in range(nb):
            other = gm[jp:jp + 1, :]
            beats = (other > gm) | ((other == gm) & (jp < j_io))
            cnt = cnt + beats.astype(jnp.int32)
        sel = valid & (jnp.abs(gate) < jnp.inf) & (cnt < MOBA_TOPK)
        cfar = cfar_ref[2 * hp + h]
        am = jnp.where(sel, jnp.where(is_prev, 0.0, cfar), NEG_INF)
        for i in range(nb):
            am_scr[h, i] = am[:, i * MOBA_BLOCK:(i + 1) * MOBA_BLOCK]

    def q_block(i, carry):
        qt = qt_scr[i]
        jprev = jnp.maximum(i - 1, 0)
        out_t = None
        for h in range(2):
            s = jnp.dot(kh_scr[h, i], qt, preferred_element_type=F32) + bias_ref[h, 0]
            m = jnp.max(s, axis=0, keepdims=True)
            p = jnp.exp(s - m)
            l = jnp.sum(p, axis=0, keepdims=True)
            acc = jnp.dot(vt_scr[h, i], p.astype(BF16), preferred_element_type=F32)

            s = (jnp.dot(kh_scr[h, jprev], qt, preferred_element_type=F32) + bias_ref[h, 1]
                 + am_scr[h, i, pl.ds(jprev, 1), :])
            m, l, acc = _online_update(s, vt_scr[h, jprev], m, l, acc)

            def far_block(j, c, h=h):
                s = (jnp.dot(kh_scr[h, j], qt, preferred_element_type=F32)
                     + am_scr[h, i, pl.ds(j, 1), :])
                return _online_update(s, vt_scr[h, j], *c)

            m, l, acc = lax.fori_loop(0, jprev, far_block, (m, l, acc))
            o_h = acc / l
            out_t = o_h if out_t is None else out_t + o_h
        o_ref[pl.ds(pl.multiple_of(i * MOBA_BLOCK, MOBA_BLOCK), MOBA_BLOCK), :] = out_t.T.astype(BF16)
        return carry

    lax.fori_loop(0, nb, q_block, 0)


def _moba(proj, kmean, bias_moba, cfar):
    bsz, seq, _ = proj.shape
    nb = seq // MOBA_BLOCK
    n_pairs = W_A // LANES
    kern = functools.partial(_moba_kernel, nb=nb)
    return pl.pallas_call(
        kern,
        grid=(bsz, n_pairs),
        in_specs=[
            pl.BlockSpec(memory_space=pltpu.SMEM),
            pl.BlockSpec((None, seq, LANES), lambda b, hp: (b, 0, hp)),
            pl.BlockSpec((None, seq, LANES), lambda b, hp: (b, 0, n_pairs + hp)),
            pl.BlockSpec((None, seq, LANES), lambda b, hp: (b, 0, 2 * n_pairs + hp)),
            pl.BlockSpec((None, nb, LANES), lambda b, hp: (b, 0, hp)),
            pl.BlockSpec((2, 2, MOBA_BLOCK, MOBA_BLOCK), lambda b, hp: (hp, 0, 0, 0)),
        ],
        out_specs=pl.BlockSpec((None, seq, LANES), lambda b, hp: (b, 0, hp)),
        out_shape=jax.ShapeDtypeStruct((bsz, seq, W_A), BF16),
        scratch_shapes=[
            pltpu.VMEM((nb, LANES, MOBA_BLOCK), BF16),
            pltpu.VMEM((2, nb, MOBA_BLOCK, LANES), BF16),
            pltpu.VMEM((2, nb, LANES, MOBA_BLOCK), BF16),
            pltpu.VMEM((2, nb, nb, MOBA_BLOCK), F32),
        ],
        compiler_params=pltpu.CompilerParams(
            dimension_semantics=("arbitrary", "arbitrary"), vmem_limit_bytes=VMEM_LIMIT),
        name="moba",
    )(cfar, proj, proj, proj, kmean, bias_moba)


def _swa_kernel(sink_ref, q_ref, k_ref, v_ref, bias_ref, o_ref, qt_scr, k_scr, vt_scr, *, nbq):
    w = SWA_WINDOW
    lane = lax.broadcasted_iota(jnp.int32, (1, LANES), 1)
    first_half = lane < HEAD_DIM
    row = lax.broadcasted_iota(jnp.int32, (LANES, 1), 0)
    first_rows = row < HEAD_DIM
    n_pairs = W_QB // LANES

    for n in range(nbq):
        rows = slice(n * w, (n + 1) * w)
        for pr in range(n_pairs):
            q_n = q_ref[rows, pr * LANES:(pr + 1) * LANES].astype(F32) * SCALE
            qt_scr[pr, n] = q_n.T.astype(BF16)
        k_n = k_ref[rows, :].astype(F32)
        k_sw = pltpu.roll(k_n, HEAD_DIM, axis=1)
        k_scr[0, 0, n] = jnp.where(first_half, k_n, 0.0).astype(BF16)
        k_scr[0, 1, n] = jnp.where(first_half, 0.0, k_sw).astype(BF16)
        k_scr[1, 0, n] = jnp.where(first_half, k_sw, 0.0).astype(BF16)
        k_scr[1, 1, n] = jnp.where(first_half, 0.0, k_n).astype(BF16)
        v_n = v_ref[rows, :].astype(F32)
        v_t = v_n.T
        v_sw_t = pltpu.roll(v_n, HEAD_DIM, axis=1).T
        vt_scr[0, 0, n] = jnp.where(first_rows, v_t, 0.0).astype(BF16)
        vt_scr[0, 1, n] = jnp.where(first_rows, 0.0, v_sw_t).astype(BF16)
        vt_scr[1, 0, n] = jnp.where(first_rows, v_sw_t, 0.0).astype(BF16)
        vt_scr[1, 1, n] = jnp.where(first_rows, 0.0, v_t).astype(BF16)

    def q_block(n, carry):
        nprev = jnp.maximum(n - 1, 0)
        prev_mask = jnp.where(n > 0, 0.0, NEG_INF)
        for pr in range(n_pairs):
            qt = qt_scr[pr, n]
            g = (pr * 2) // GQA_GROUP
            out_t = None
            for pos in range(2):
                head = pr * 2 + pos
                s_prev = (jnp.dot(k_scr[g, pos, nprev], qt, preferred_element_type=F32)
                          + bias_ref[head, 0:w, :] + prev_mask)
                s_own = (jnp.dot(k_scr[g, pos, n], qt, preferred_element_type=F32)
                         + bias_ref[head, w:2 * w, :])
                sink = sink_ref[head]
                m = jnp.maximum(jnp.maximum(jnp.max(s_prev, axis=0, keepdims=True),
                                            jnp.max(s_own, axis=0, keepdims=True)), sink)
                p_prev = jnp.exp(s_prev - m)
                p_own = jnp.exp(s_own - m)
                l = (jnp.sum(p_prev, axis=0, keepdims=True) + jnp.sum(p_own, axis=0, keepdims=True)
                     + jnp.exp(sink - m))
                o_h = (jnp.dot(vt_scr[g, pos, nprev], p_prev.astype(BF16), preferred_element_type=F32)
                       + jnp.dot(vt_scr[g, pos, n], p_own.astype(BF16), preferred_element_type=F32)) / l
                out_t = o_h if out_t is None else out_t + o_h
            o_ref[pl.ds(pl.multiple_of(n * w, w), w), pr * LANES:(pr + 1) * LANES] = out_t.T.astype(BF16)
        return carry

    lax.fori_loop(0, nbq, q_block, 0)


def _swa(proj, bias_swa, sinks_l):
    bsz, seq, _ = proj.shape
    nbq = seq // SWA_WINDOW
    kern = functools.partial(_swa_kernel, nbq=nbq)
    q_col = (3 * W_A) // W_QB
    k_col = (3 * W_A + W_QB) // W_KVB
    return pl.pallas_call(
        kern,
        grid=(bsz,),
        in_specs=[
            pl.BlockSpec(memory_space=pltpu.SMEM),
            pl.BlockSpec((None, seq, W_QB), lambda b: (b, 0, q_col)),
            pl.BlockSpec((None, seq, W_KVB), lambda b: (b, 0, k_col)),
            pl.BlockSpec((None, seq, W_KVB), lambda b: (b, 0, k_col + 1)),
            pl.BlockSpec((N_HEADS_SWA, 2 * SWA_WINDOW, SWA_WINDOW), lambda b: (0, 0, 0)),
        ],
        out_specs=pl.BlockSpec((None, seq, W_QB), lambda b: (b, 0, 0)),
        out_shape=jax.ShapeDtypeStruct((bsz, seq, W_QB), BF16),
        scratch_shapes=[
            pltpu.VMEM((W_QB // LANES, nbq, LANES, SWA_WINDOW), BF16),
            pltpu.VMEM((N_KV_SWA, 2, nbq, SWA_WINDOW, LANES), BF16),
            pltpu.VMEM((N_KV_SWA, 2, nbq, LANES, SWA_WINDOW), BF16),
        ],
        compiler_params=pltpu.CompilerParams(
            dimension_semantics=("arbitrary",), vmem_limit_bytes=VMEM_LIMIT),
        name="swa",
    )(sinks_l, proj, proj, proj, bias_swa)


def _mix_mlp_kernel(a_ref, b_ref, x_ref, mod_ref, g_ref, wo_ref, w1_ref, b1_ref, w2_ref, b2_ref, o_ref):
    y = (jnp.dot(a_ref[...], wo_ref[0:W_A, :], preferred_element_type=F32)
         + jnp.dot(b_ref[...], wo_ref[W_A:, :], preferred_element_type=F32))
    x = x_ref[...] + mod_ref[2:3, :] * _rms(y, g_ref[1:2, :])
    h = _rms(x, g_ref[2:3, :]) * (1.0 + mod_ref[4:5, :]) + mod_ref[3:4, :]
    hb = h.astype(BF16)
    acc = jnp.zeros(x.shape, F32)
    for c0 in range(0, D_FF, FF_CHUNK):
        u = jnp.dot(hb, w1_ref[:, c0:c0 + FF_CHUNK], preferred_element_type=F32) + b1_ref[:, c0:c0 + FF_CHUNK]
        u = jnp.square(jnp.maximum(u, 0.0))
        acc = acc + jnp.dot(u.astype(BF16), w2_ref[c0:c0 + FF_CHUNK, :], preferred_element_type=F32)
    y2 = acc + b2_ref[...]
    o_ref[...] = x + mod_ref[5:6, :] * _rms(y2, g_ref[3:4, :])


def _mix_mlp(attn_a, attn_b, x, mod_l, gains_l, w_out_l, w1_l, b1_l, w2_l, b2_l):
    bsz, seq, d = x.shape
    const = lambda shape: pl.BlockSpec(shape, lambda b, t: (0,) * len(shape), pipeline_mode=pl.Buffered(1))
    return pl.pallas_call(
        _mix_mlp_kernel,
        grid=(bsz, seq // ROW_TILE),
        in_specs=[
            pl.BlockSpec((None, ROW_TILE, W_A), lambda b, t: (b, t, 0)),
            pl.BlockSpec((None, ROW_TILE, W_QB), lambda b, t: (b, t, 0)),
            pl.BlockSpec((None, ROW_TILE, d), lambda b, t: (b, t, 0)),
            pl.BlockSpec((None, 6, d), lambda b, t: (b, 0, 0)),
            pl.BlockSpec((4, d), lambda b, t: (0, 0)),
            const((d, d)),
            const((d, D_FF)),
            const((1, D_FF)),
            const((D_FF, d)),
            const((1, d)),
        ],
        out_specs=pl.BlockSpec((None, ROW_TILE, d), lambda b, t: (b, t, 0)),
        out_shape=jax.ShapeDtypeStruct((bsz, seq, d), F32),
        compiler_params=pltpu.CompilerParams(
            dimension_semantics=("arbitrary", "arbitrary"), vmem_limit_bytes=VMEM_LIMIT),
        name="mix_mlp",
    )(attn_a, attn_b, x, mod_l, gains_l, w_out_l, w1_l, b1_l, w2_l, b2_l)


def kernel(x, c, w_in, w_out, sinks, rel_bias, w_ada, b_ada, norm_gains, w1, b1, w2, b2):
    bsz, seq, d = x.shape
    depth = w_in.shape[0]
    assert d == D_MODEL and seq % ROW_TILE == 0 and ROW_TILE % MOBA_BLOCK == 0
    assert seq // MOBA_BLOCK == 8, "the block-gate matmul packs 8 block means per bf16 tile row group"

    mod = _adaln(c, w_ada, b_ada).reshape(depth, bsz, 6, d)
    bias_moba, bias_swa = _bias_tiles(rel_bias)
    cfar = rel_bias[N_BUCKETS - 1, :N_HEADS_MOBA]

    w_in_b = w_in.astype(BF16)
    w_out_b = w_out.astype(BF16)
    w1_b = w1.astype(BF16)
    w2_b = w2.astype(BF16)

    for l in range(depth):
        proj, kmean = _inproj(x, mod[l], norm_gains[l], w_in_b[l])
        attn_a = _moba(proj, kmean, bias_moba, cfar)
        attn_b = _swa(proj, bias_swa, sinks[l])
        x = _mix_mlp(attn_a, attn_b, x, mod[l], norm_gains[l], w_out_b[l], w1_b[l],
                     b1[l].reshape(1, D_FF), w2_b[l], b2[l].reshape(1, d))
    return x
```

```python
import functools
import math

import jax
import jax.numpy as jnp
from jax import lax
from jax.experimental import pallas as pl
from jax.experimental.pallas import tpu as pltpu

D_MODEL = 1024
HEAD_DIM = 64
N_HEADS_MOBA = 8
N_HEADS_SWA = 8
N_KV_SWA = 2
GQA_GROUP = N_HEADS_SWA // N_KV_SWA
MOBA_BLOCK = 256
MOBA_TOPK = 3
SWA_WINDOW = 128
N_BUCKETS = 32
MAX_DISTANCE = 128
D_FF = 4 * D_MODEL
EPS = 1e-6
W_A = N_HEADS_MOBA * HEAD_DIM
W_QB = N_HEADS_SWA * HEAD_DIM
W_KVB = N_KV_SWA * HEAD_DIM
D_IN = 3 * W_A + W_QB + 2 * W_KVB
SCALE = HEAD_DIM ** -0.5

LANES = 128
VMEM_LIMIT = 56 * 1024 * 1024
ROW_TILE = 512
FF_CHUNK = 1024
ADA_COL_TILE = 2048

F32 = jnp.float32
BF16 = jnp.bfloat16
NEG_INF = float("-inf")


def _rms(x, g):
    return x * lax.rsqrt(jnp.mean(x * x, axis=-1, keepdims=True) + EPS) * g


def _t5_bucket(dist):
    n = jnp.maximum(dist, 0)
    max_exact = N_BUCKETS // 2
    nf = jnp.maximum(n, 1).astype(F32)
    large = max_exact + (jnp.log(nf / max_exact) / math.log(MAX_DISTANCE / max_exact)
                         * (N_BUCKETS - max_exact)).astype(jnp.int32)
    large = jnp.minimum(large, N_BUCKETS - 1)
    return jnp.where(n < max_exact, n, large)


def _bias_from_dist(dist, rb_ref, head):
    bucket = _t5_bucket(dist)
    val = jnp.zeros(dist.shape, F32)
    for b in range(N_BUCKETS):
        val = jnp.where(bucket == b, rb_ref[b, head], val)
    return val


def _adaln_kernel(c_ref, w_ref, b_ref, o_ref):
    c = c_ref[...]
    c_act = c * (1.0 / (1.0 + jnp.exp(-c)))
    o_ref[...] = jnp.dot(c_act.astype(BF16), w_ref[...].astype(BF16),
                         preferred_element_type=F32) + b_ref[...]


def _adaln(c, w_ada, b_ada):
    depth, d, n = w_ada.shape
    bsz = c.shape[0]
    return pl.pallas_call(
        _adaln_kernel,
        grid=(depth, n // ADA_COL_TILE),
        in_specs=[
            pl.BlockSpec((bsz, d), lambda l, j: (0, 0)),
            pl.BlockSpec((None, d, ADA_COL_TILE), lambda l, j: (l, 0, j)),
            pl.BlockSpec((None, 1, ADA_COL_TILE), lambda l, j: (l, 0, j)),
        ],
        out_specs=pl.BlockSpec((None, bsz, ADA_COL_TILE), lambda l, j: (l, 0, j)),
        out_shape=jax.ShapeDtypeStruct((depth, bsz, n), F32),
        compiler_params=pltpu.CompilerParams(
            dimension_semantics=("arbitrary", "arbitrary"), vmem_limit_bytes=VMEM_LIMIT),
        name="adaln",
    )(c, w_ada, b_ada.reshape(depth, 1, n))


def _moba_bias_kernel(rb_ref, o_ref):
    h = pl.program_id(0)
    kk = lax.broadcasted_iota(jnp.int32, (MOBA_BLOCK, MOBA_BLOCK), 0)
    qq = lax.broadcasted_iota(jnp.int32, (MOBA_BLOCK, MOBA_BLOCK), 1)
    d_own = qq - kk
    o_ref[0] = jnp.where(d_own >= 0, _bias_from_dist(d_own, rb_ref, h), NEG_INF)
    o_ref[1] = _bias_from_dist(d_own + MOBA_BLOCK, rb_ref, h)


def _swa_bias_kernel(rb_ref, o_ref):
    h = pl.program_id(0) + N_HEADS_MOBA
    jj = lax.broadcasted_iota(jnp.int32, (2 * SWA_WINDOW, SWA_WINDOW), 0)
    qq = lax.broadcasted_iota(jnp.int32, (2 * SWA_WINDOW, SWA_WINDOW), 1)
    dist = qq + SWA_WINDOW - jj
    allowed = (dist >= 0) & (dist < SWA_WINDOW)
    o_ref[...] = jnp.where(allowed, _bias_from_dist(dist, rb_ref, h), NEG_INF)


def _bias_tiles(rel_bias):
    smem = pl.BlockSpec(memory_space=pltpu.SMEM)
    moba = pl.pallas_call(
        _moba_bias_kernel,
        grid=(N_HEADS_MOBA,),
        in_specs=[smem],
        out_specs=pl.BlockSpec((None, 2, MOBA_BLOCK, MOBA_BLOCK), lambda h: (h, 0, 0, 0)),
        out_shape=jax.ShapeDtypeStruct((N_HEADS_MOBA, 2, MOBA_BLOCK, MOBA_BLOCK), F32),
        compiler_params=pltpu.CompilerParams(dimension_semantics=("arbitrary",)),
        name="moba_bias",
    )(rel_bias)
    swa = pl.pallas_call(
        _swa_bias_kernel,
        grid=(N_HEADS_SWA,),
        in_specs=[smem],
        out_specs=pl.BlockSpec((None, 2 * SWA_WINDOW, SWA_WINDOW), lambda h: (h, 0, 0)),
        out_shape=jax.ShapeDtypeStruct((N_HEADS_SWA, 2 * SWA_WINDOW, SWA_WINDOW), F32),
        compiler_params=pltpu.CompilerParams(dimension_semantics=("arbitrary",)),
        name="swa_bias",
    )(rel_bias)
    return moba, swa


def _inproj_kernel(x_ref, mod_ref, g_ref, w_ref, proj_ref, kmean_ref):
    t = pl.program_id(1)
    h = _rms(x_ref[...], g_ref[0:1, :]) * (1.0 + mod_ref[1:2, :]) + mod_ref[0:1, :]
    hb = h.astype(BF16)
    blocks_per_tile = ROW_TILE // MOBA_BLOCK
    for c0 in range(0, D_IN, W_A):
        c1 = min(c0 + W_A, D_IN)
        p = jnp.dot(hb, w_ref[:, c0:c1], preferred_element_type=F32)
        proj_ref[:, c0:c1] = p.astype(BF16)
        if c0 == W_A:
            for r in range(blocks_per_tile):
                kmean_ref[pl.ds(t * blocks_per_tile + r, 1), :] = jnp.mean(
                    p[r * MOBA_BLOCK:(r + 1) * MOBA_BLOCK], axis=0, keepdims=True)


def _inproj(x, mod_l, gains_l, w_in_l):
    bsz, seq, d = x.shape
    nb = seq // MOBA_BLOCK
    return pl.pallas_call(
        _inproj_kernel,
        grid=(bsz, seq // ROW_TILE),
        in_specs=[
            pl.BlockSpec((None, ROW_TILE, d), lambda b, t: (b, t, 0)),
            pl.BlockSpec((None, 6, d), lambda b, t: (b, 0, 0)),
            pl.BlockSpec((4, d), lambda b, t: (0, 0)),
            pl.BlockSpec((d, D_IN), lambda b, t: (0, 0)),
        ],
        out_specs=[
            pl.BlockSpec((None, ROW_TILE, D_IN), lambda b, t: (b, t, 0)),
            pl.BlockSpec((None, nb, W_A), lambda b, t: (b, 0, 0)),
        ],
        out_shape=[
            jax.ShapeDtypeStruct((bsz, seq, D_IN), BF16),
            jax.ShapeDtypeStruct((bsz, nb, W_A), F32),
        ],
        compiler_params=pltpu.CompilerParams(
            dimension_semantics=("arbitrary", "arbitrary"), vmem_limit_bytes=VMEM_LIMIT),
        name="inproj",
    )(x, mod_l, gains_l, w_in_l)


def _moba_kernel(cfar_ref, q_ref, k_ref, v_ref, km_ref, bias_ref, o_ref,
                 qt_scr, kh_scr, vt_scr, am_scr, *, nb):
    hp = pl.program_id(1)
    seq = nb * MOBA_BLOCK
    lane = lax.broadcasted_iota(jnp.int32, (1, LANES), 1)
    first_head = lane < HEAD_DIM
    row = lax.broadcasted_iota(jnp.int32, (LANES, 1), 0)
    first_rows = row < HEAD_DIM

    ones_row0 = jnp.where(row == 0, 1.0, 0.0)
    ones_row64 = jnp.where(row == HEAD_DIM, 1.0, 0.0)
    for i in range(nb):
        rows = slice(i * MOBA_BLOCK, (i + 1) * MOBA_BLOCK)
        qt_scr[i] = (q_ref[rows, :].astype(F32) * SCALE).T.astype(BF16)
        k_i = k_ref[rows, :].astype(F32)
        kh_scr[0, rows, :] = jnp.where(first_head, k_i, 0.0).astype(BF16)
        kh_scr[1, rows, :] = jnp.where(first_head, 0.0, k_i).astype(BF16)
        v_t = v_ref[rows, :].astype(F32).T
        vt_scr[0, :, rows] = jnp.where(first_rows, v_t, ones_row64).astype(BF16)
        vt_scr[1, :, rows] = jnp.where(first_rows, ones_row0, v_t).astype(BF16)

    j_io = lax.broadcasted_iota(jnp.int32, (nb, seq), 0)
    qblk = lax.broadcasted_iota(jnp.int32, (nb, seq), 1) // MOBA_BLOCK
    valid = j_io < qblk
    is_prev = j_io == qblk - 1
    q_all = q_ref[...]
    km = km_ref[...]
    for h in range(2):
        kmh = jnp.where(first_head, km, 0.0) if h == 0 else jnp.where(first_head, 0.0, km)
        p0 = kmh.astype(BF16)
        r1 = kmh - p0.astype(F32)
        p1 = r1.astype(BF16)
        p2 = (r1 - p1.astype(F32)).astype(BF16)
        lhs = jnp.concatenate([p0, p1, p2, jnp.zeros_like(p0)], axis=0)
        g4 = lax.dot_general(lhs, q_all, (((1,), (1,)), ((), ())), preferred_element_type=F32)
        gate = g4[0:nb] + g4[nb:2 * nb] + g4[2 * nb:3 * nb]
        gm = jnp.where(valid, gate, NEG_INF)
        cnt = jnp.zeros((nb, seq), jnp.int32)
        for jp in range(nb):
            other = gm[jp:jp + 1, :]
            beats = (other > gm) | ((other == gm) & (jp < j_io))
            cnt = cnt + beats.astype(jnp.int32)
        sel = valid & (jnp.abs(gate) < jnp.inf) & (cnt < MOBA_TOPK)
        cfar = cfar_ref[2 * hp + h]
        am = jnp.where(sel, jnp.where(is_prev, 0.0, cfar), NEG_INF)
        for i in range(nb):
            am_scr[h, i] = am[:, i * MOBA_BLOCK:(i + 1) * MOBA_BLOCK]

    for i in range(nb):
        qt = qt_scr[i]
        n_keys = (i + 1) * MOBA_BLOCK
        acc = []
        for h in range(2):
            s = jnp.dot(kh_scr[h, 0:n_keys, :], qt, preferred_element_type=F32)
            tiles = []
            for j in range(i + 1):
                s_j = s[j * MOBA_BLOCK:(j + 1) * MOBA_BLOCK]
                if j == i:
                    s_j = s_j + bias_ref[h, 0]
                elif j == i - 1:
                    s_j = s_j + bias_ref[h, 1] + am_scr[h, i, j:j + 1, :]
                else:
                    s_j = s_j + am_scr[h, i, j:j + 1, :]
                tiles.append(s_j)
            m = functools.reduce(jnp.maximum, [jnp.max(t, axis=0, keepdims=True) for t in tiles])
            p = jnp.concatenate([jnp.exp(t - m).astype(BF16) for t in tiles], axis=0)
            acc.append(jnp.dot(vt_scr[h, :, 0:n_keys], p, preferred_element_type=F32))
        l0 = acc[0][HEAD_DIM:HEAD_DIM + 1, :]
        l1 = acc[1][0:1, :]
        out_t = jnp.where(first_rows, acc[0] * (1.0 / l0), acc[1] * (1.0 / l1))
        o_ref[i * MOBA_BLOCK:(i + 1) * MOBA_BLOCK, :] = out_t.T.astype(BF16)


def _moba(proj, kmean, bias_moba, cfar):
    bsz, seq, _ = proj.shape
    nb = seq // MOBA_BLOCK
    n_pairs = W_A // LANES
    kern = functools.partial(_moba_kernel, nb=nb)
    return pl.pallas_call(
        kern,
        grid=(bsz, n_pairs),
        in_specs=[
            pl.BlockSpec(memory_space=pltpu.SMEM),
            pl.BlockSpec((None, seq, LANES), lambda b, hp: (b, 0, hp)),
            pl.BlockSpec((None, seq, LANES), lambda b, hp: (b, 0, n_pairs + hp)),
            pl.BlockSpec((None, seq, LANES), lambda b, hp: (b, 0, 2 * n_pairs + hp)),
            pl.BlockSpec((None, nb, LANES), lambda b, hp: (b, 0, hp)),
            pl.BlockSpec((2, 2, MOBA_BLOCK, MOBA_BLOCK), lambda b, hp: (hp, 0, 0, 0)),
        ],
        out_specs=pl.BlockSpec((None, seq, LANES), lambda b, hp: (b, 0, hp)),
        out_shape=jax.ShapeDtypeStruct((bsz, seq, W_A), BF16),
        scratch_shapes=[
            pltpu.VMEM((nb, LANES, MOBA_BLOCK), BF16),
            pltpu.VMEM((2, seq, LANES), BF16),
            pltpu.VMEM((2, LANES, seq), BF16),
            pltpu.VMEM((2, nb, nb, MOBA_BLOCK), F32),
        ],
        compiler_params=pltpu.CompilerParams(
            dimension_semantics=("arbitrary", "arbitrary"), vmem_limit_bytes=VMEM_LIMIT),
        name="moba",
    )(cfar, proj, proj, proj, kmean, bias_moba)


def _swa_kernel(sink_ref, q_ref, k_ref, v_ref, bias_ref, o_ref, qt_scr, k_scr, vt_scr, *, nbq):
    w = SWA_WINDOW
    lane = lax.broadcasted_iota(jnp.int32, (1, LANES), 1)
    first_half = lane < HEAD_DIM
    row = lax.broadcasted_iota(jnp.int32, (LANES, 1), 0)
    first_rows = row < HEAD_DIM
    n_pairs = W_QB // LANES

    for n in range(nbq):
        rows = slice(n * w, (n + 1) * w)
        for pr in range(n_pairs):
            q_n = q_ref[rows, pr * LANES:(pr + 1) * LANES].astype(F32) * SCALE
            qt_scr[pr, n] = q_n.T.astype(BF16)
        k_n = k_ref[rows, :].astype(F32)
        k_sw = pltpu.roll(k_n, HEAD_DIM, axis=1)
        k_scr[0, 0, n] = jnp.where(first_half, k_n, 0.0).astype(BF16)
        k_scr[0, 1, n] = jnp.where(first_half, 0.0, k_sw).astype(BF16)
        k_scr[1, 0, n] = jnp.where(first_half, k_sw, 0.0).astype(BF16)
        k_scr[1, 1, n] = jnp.where(first_half, 0.0, k_n).astype(BF16)
        v_n = v_ref[rows, :].astype(F32)
        v_t = v_n.T
        v_sw_t = pltpu.roll(v_n, HEAD_DIM, axis=1).T
        vt_scr[0, 0, n] = jnp.where(first_rows, v_t, 0.0).astype(BF16)
        vt_scr[0, 1, n] = jnp.where(first_rows, 0.0, v_sw_t).astype(BF16)
        vt_scr[1, 0, n] = jnp.where(first_rows, v_sw_t, 0.0).astype(BF16)
        vt_scr[1, 1, n] = jnp.where(first_rows, 0.0, v_t).astype(BF16)

    def q_block(n, carry):
        nprev = jnp.maximum(n - 1, 0)
        prev_mask = jnp.where(n > 0, 0.0, NEG_INF)
        for pr in range(n_pairs):
            qt = qt_scr[pr, n]
            g = (pr * 2) // GQA_GROUP
            out_t = None
            for pos in range(2):
                head = pr * 2 + pos
                s_prev = (jnp.dot(k_scr[g, pos, nprev], qt, preferred_element_type=F32)
                          + bias_ref[head, 0:w, :] + prev_mask)
                s_own = (jnp.dot(k_scr[g, pos, n], qt, preferred_element_type=F32)
                         + bias_ref[head, w:2 * w, :])
                sink = sink_ref[head]
                m = jnp.maximum(jnp.maximum(jnp.max(s_prev, axis=0, keepdims=True),
                                            jnp.max(s_own, axis=0, keepdims=True)), sink)
                p_prev = jnp.exp(s_prev - m)
                p_own = jnp.exp(s_own - m)
                l = (jnp.sum(p_prev, axis=0, keepdims=True) + jnp.sum(p_own, axis=0, keepdims=True)
                     + jnp.exp(sink - m))
                o_h = (jnp.dot(vt_scr[g, pos, nprev], p_prev.astype(BF16), preferred_element_type=F32)
                       + jnp.dot(vt_scr[g, pos, n], p_own.astype(BF16), preferred_element_type=F32)) / l
                out_t = o_h if out_t is None else out_t + o_h
            o_ref[pl.ds(pl.multiple_of(n * w, w), w), pr * LANES:(pr + 1) * LANES] = out_t.T.astype(BF16)
        return carry

    lax.fori_loop(0, nbq, q_block, 0)


def _swa(proj, bias_swa, sinks_l):
    bsz, seq, _ = proj.shape
    nbq = seq // SWA_WINDOW
    kern = functools.partial(_swa_kernel, nbq=nbq)
    q_col = (3 * W_A) // W_QB
    k_col = (3 * W_A + W_QB) // W_KVB
    return pl.pallas_call(
        kern,
        grid=(bsz,),
        in_specs=[
            pl.BlockSpec(memory_space=pltpu.SMEM),
            pl.BlockSpec((None, seq, W_QB), lambda b: (b, 0, q_col)),
            pl.BlockSpec((None, seq, W_KVB), lambda b: (b, 0, k_col)),
            pl.BlockSpec((None, seq, W_KVB), lambda b: (b, 0, k_col + 1)),
            pl.BlockSpec((N_HEADS_SWA, 2 * SWA_WINDOW, SWA_WINDOW), lambda b: (0, 0, 0)),
        ],
        out_specs=pl.BlockSpec((None, seq, W_QB), lambda b: (b, 0, 0)),
        out_shape=jax.ShapeDtypeStruct((bsz, seq, W_QB), BF16),
        scratch_shapes=[
            pltpu.VMEM((W_QB // LANES, nbq, LANES, SWA_WINDOW), BF16),
            pltpu.VMEM((N_KV_SWA, 2, nbq, SWA_WINDOW, LANES), BF16),
            pltpu.VMEM((N_KV_SWA, 2, nbq, LANES, SWA_WINDOW), BF16),
        ],
        compiler_params=pltpu.CompilerParams(
            dimension_semantics=("arbitrary",), vmem_limit_bytes=VMEM_LIMIT),
        name="swa",
    )(sinks_l, proj, proj, proj, bias_swa)


def _mix_mlp_kernel(a_ref, b_ref, x_ref, mod_ref, g_ref, wo_ref, w1_ref, b1_ref, w2_ref, b2_ref, o_ref):
    y = (jnp.dot(a_ref[...], wo_ref[0:W_A, :], preferred_element_type=F32)
         + jnp.dot(b_ref[...], wo_ref[W_A:, :], preferred_element_type=F32))
    x = x_ref[...] + mod_ref[2:3, :] * _rms(y, g_ref[1:2, :])
    h = _rms(x, g_ref[2:3, :]) * (1.0 + mod_ref[4:5, :]) + mod_ref[3:4, :]
    hb = h.astype(BF16)
    acc = jnp.zeros(x.shape, F32)
    for c0 in range(0, D_FF, FF_CHUNK):
        u = jnp.dot(hb, w1_ref[:, c0:c0 + FF_CHUNK], preferred_element_type=F32) + b1_ref[:, c0:c0 + FF_CHUNK]
        u = jnp.square(jnp.maximum(u, 0.0))
        acc = acc + jnp.dot(u.astype(BF16), w2_ref[c0:c0 + FF_CHUNK, :], preferred_element_type=F32)
    y2 = acc + b2_ref[...]
    o_ref[...] = x + mod_ref[5:6, :] * _rms(y2, g_ref[3:4, :])


def _mix_mlp(attn_a, attn_b, x, mod_l, gains_l, w_out_l, w1_l, b1_l, w2_l, b2_l):
    bsz, seq, d = x.shape
    const = lambda shape: pl.BlockSpec(shape, lambda b, t: (0,) * len(shape), pipeline_mode=pl.Buffered(1))
    return pl.pallas_call(
        _mix_mlp_kernel,
        grid=(bsz, seq // ROW_TILE),
        in_specs=[
            pl.BlockSpec((None, ROW_TILE, W_A), lambda b, t: (b, t, 0)),
            pl.BlockSpec((None, ROW_TILE, W_QB), lambda b, t: (b, t, 0)),
            pl.BlockSpec((None, ROW_TILE, d), lambda b, t: (b, t, 0)),
            pl.BlockSpec((None, 6, d), lambda b, t: (b, 0, 0)),
            pl.BlockSpec((4, d), lambda b, t: (0, 0)),
            const((d, d)),
            const((d, D_FF)),
            const((1, D_FF)),
            const((D_FF, d)),
            const((1, d)),
        ],
        out_specs=pl.BlockSpec((None, ROW_TILE, d), lambda b, t: (b, t, 0)),
        out_shape=jax.ShapeDtypeStruct((bsz, seq, d), F32),
        compiler_params=pltpu.CompilerParams(
            dimension_semantics=("arbitrary", "arbitrary"), vmem_limit_bytes=VMEM_LIMIT),
        name="mix_mlp",
    )(attn_a, attn_b, x, mod_l, gains_l, w_out_l, w1_l, b1_l, w2_l, b2_l)


def kernel(x, c, w_in, w_out, sinks, rel_bias, w_ada, b_ada, norm_gains, w1, b1, w2, b2):
    bsz, seq, d = x.shape
    depth = w_in.shape[0]
    assert d == D_MODEL and seq % ROW_TILE == 0 and ROW_TILE % MOBA_BLOCK == 0
    assert seq // MOBA_BLOCK == 8, "the block-gate matmul packs 8 block means per bf16 tile row group"

    mod = _adaln(c, w_ada, b_ada).reshape(depth, bsz, 6, d)
    bias_moba, bias_swa = _bias_tiles(rel_bias)
    cfar = rel_bias[N_BUCKETS - 1, :N_HEADS_MOBA]

    w_in_b = w_in.astype(BF16)
    w_out_b = w_out.astype(BF16)
    w1_b = w1.astype(BF16)
    w2_b = w2.astype(BF16)

    for l in range(depth):
        proj, kmean = _inproj(x, mod[l], norm_gains[l], w_in_b[l])
        attn_a = _moba(proj, kmean, bias_moba, cfar)
        attn_b = _swa(proj, bias_swa, sinks[l])
        x = _mix_mlp(attn_a, attn_b, x, mod[l], norm_gains[l], w_out_b[l], w1_b[l],
                     b1[l].reshape(1, D_FF), w2_b[l], b2[l].reshape(1, d))
    return x
```

```python
import functools
import math

import jax
import jax.numpy as jnp
from jax import lax
from jax.experimental import pallas as pl
from jax.experimental.pallas import tpu as pltpu

D_MODEL = 1024
HEAD_DIM = 64
N_HEADS_MOBA = 8
N_HEADS_SWA = 8
N_KV_SWA = 2
GQA_GROUP = N_HEADS_SWA // N_KV_SWA
MOBA_BLOCK = 256
MOBA_TOPK = 3
SWA_WINDOW = 128
N_BUCKETS = 32
MAX_DISTANCE = 128
D_FF = 4 * D_MODEL
EPS = 1e-6
W_A = N_HEADS_MOBA * HEAD_DIM
W_QB = N_HEADS_SWA * HEAD_DIM
W_KVB = N_KV_SWA * HEAD_DIM
D_IN = 3 * W_A + W_QB + 2 * W_KVB
SCALE = HEAD_DIM ** -0.5

LANES = 128
VMEM_LIMIT = 56 * 1024 * 1024
ROW_TILE = 512
FF_CHUNK = 1024
ADA_COL_TILE = 2048

F32 = jnp.float32
BF16 = jnp.bfloat16
NEG_INF = float("-inf")
LOG2E = 1.4426950408889634
MASKED = -1e30
VT_ROWS = 80


def _rms(x, g):
    return x * lax.rsqrt(jnp.mean(x * x, axis=-1, keepdims=True) + EPS) * g


def _t5_bucket(dist):
    n = jnp.maximum(dist, 0)
    max_exact = N_BUCKETS // 2
    nf = jnp.maximum(n, 1).astype(F32)
    large = max_exact + (jnp.log(nf / max_exact) / math.log(MAX_DISTANCE / max_exact)
                         * (N_BUCKETS - max_exact)).astype(jnp.int32)
    large = jnp.minimum(large, N_BUCKETS - 1)
    return jnp.where(n < max_exact, n, large)


def _bias_from_dist(dist, rb_ref, head):
    bucket = _t5_bucket(dist)
    val = jnp.zeros(dist.shape, F32)
    for b in range(N_BUCKETS):
        val = jnp.where(bucket == b, rb_ref[b, head], val)
    return val


def _adaln_kernel(c_ref, w_ref, b_ref, o_ref):
    c = c_ref[...]
    c_act = c * (1.0 / (1.0 + jnp.exp(-c)))
    o_ref[...] = jnp.dot(c_act.astype(BF16), w_ref[...].astype(BF16),
                         preferred_element_type=F32) + b_ref[...]


def _adaln(c, w_ada, b_ada):
    depth, d, n = w_ada.shape
    bsz = c.shape[0]
    return pl.pallas_call(
        _adaln_kernel,
        grid=(depth, n // ADA_COL_TILE),
        in_specs=[
            pl.BlockSpec((bsz, d), lambda l, j: (0, 0)),
            pl.BlockSpec((None, d, ADA_COL_TILE), lambda l, j: (l, 0, j)),
            pl.BlockSpec((None, 1, ADA_COL_TILE), lambda l, j: (l, 0, j)),
        ],
        out_specs=pl.BlockSpec((None, bsz, ADA_COL_TILE), lambda l, j: (l, 0, j)),
        out_shape=jax.ShapeDtypeStruct((depth, bsz, n), F32),
        compiler_params=pltpu.CompilerParams(
            dimension_semantics=("arbitrary", "arbitrary"), vmem_limit_bytes=VMEM_LIMIT),
        name="adaln",
    )(c, w_ada, b_ada.reshape(depth, 1, n))


def _moba_bias_kernel(rb_ref, o_ref):
    h = pl.program_id(0)
    kk = lax.broadcasted_iota(jnp.int32, (MOBA_BLOCK, MOBA_BLOCK), 0)
    qq = lax.broadcasted_iota(jnp.int32, (MOBA_BLOCK, MOBA_BLOCK), 1)
    d_own = qq - kk
    o_ref[0] = jnp.where(d_own >= 0, _bias_from_dist(d_own, rb_ref, h) * LOG2E, NEG_INF)
    o_ref[1] = _bias_from_dist(d_own + MOBA_BLOCK, rb_ref, h) * LOG2E


def _swa_bias_kernel(rb_ref, o_ref):
    h = pl.program_id(0) + N_HEADS_MOBA
    jj = lax.broadcasted_iota(jnp.int32, (2 * SWA_WINDOW, SWA_WINDOW), 0)
    qq = lax.broadcasted_iota(jnp.int32, (2 * SWA_WINDOW, SWA_WINDOW), 1)
    dist = qq + SWA_WINDOW - jj
    allowed = (dist >= 0) & (dist < SWA_WINDOW)
    o_ref[...] = jnp.where(allowed, _bias_from_dist(dist, rb_ref, h) * LOG2E, NEG_INF)


def _bias_tiles(rel_bias):
    smem = pl.BlockSpec(memory_space=pltpu.SMEM)
    moba = pl.pallas_call(
        _moba_bias_kernel,
        grid=(N_HEADS_MOBA,),
        in_specs=[smem],
        out_specs=pl.BlockSpec((None, 2, MOBA_BLOCK, MOBA_BLOCK), lambda h: (h, 0, 0, 0)),
        out_shape=jax.ShapeDtypeStruct((N_HEADS_MOBA, 2, MOBA_BLOCK, MOBA_BLOCK), F32),
        compiler_params=pltpu.CompilerParams(dimension_semantics=("arbitrary",)),
        name="moba_bias",
    )(rel_bias)
    swa = pl.pallas_call(
        _swa_bias_kernel,
        grid=(N_HEADS_SWA,),
        in_specs=[smem],
        out_specs=pl.BlockSpec((None, 2 * SWA_WINDOW, SWA_WINDOW),
                               lambda h: (h // GQA_GROUP, 0, h % GQA_GROUP)),
        out_shape=jax.ShapeDtypeStruct((N_KV_SWA, 2 * SWA_WINDOW, GQA_GROUP * SWA_WINDOW), F32),
        compiler_params=pltpu.CompilerParams(dimension_semantics=("arbitrary",)),
        name="swa_bias",
    )(rel_bias)
    return moba, swa


def _inproj_kernel(x_ref, mod_ref, g_ref, w_ref, proj_ref, kmean_ref):
    t = pl.program_id(1)
    h = _rms(x_ref[...], g_ref[0:1, :]) * (1.0 + mod_ref[1:2, :]) + mod_ref[0:1, :]
    hb = h.astype(BF16)
    blocks_per_tile = ROW_TILE // MOBA_BLOCK
    for c0 in range(0, D_IN, W_A):
        c1 = min(c0 + W_A, D_IN)
        p = jnp.dot(hb, w_ref[:, c0:c1], preferred_element_type=F32)
        proj_ref[:, c0:c1] = p.astype(BF16)
        if c0 == W_A:
            for r in range(blocks_per_tile):
                kmean_ref[pl.ds(t * blocks_per_tile + r, 1), :] = jnp.mean(
                    p[r * MOBA_BLOCK:(r + 1) * MOBA_BLOCK], axis=0, keepdims=True)


def _inproj(x, mod_l, gains_l, w_in_l):
    bsz, seq, d = x.shape
    nb = seq // MOBA_BLOCK
    return pl.pallas_call(
        _inproj_kernel,
        grid=(bsz, seq // ROW_TILE),
        in_specs=[
            pl.BlockSpec((None, ROW_TILE, d), lambda b, t: (b, t, 0)),
            pl.BlockSpec((None, 6, d), lambda b, t: (b, 0, 0)),
            pl.BlockSpec((4, d), lambda b, t: (0, 0)),
            pl.BlockSpec((d, D_IN), lambda b, t: (0, 0)),
        ],
        out_specs=[
            pl.BlockSpec((None, ROW_TILE, D_IN), lambda b, t: (b, t, 0)),
            pl.BlockSpec((None, nb, W_A), lambda b, t: (b, 0, 0)),
        ],
        out_shape=[
            jax.ShapeDtypeStruct((bsz, seq, D_IN), BF16),
            jax.ShapeDtypeStruct((bsz, nb, W_A), F32),
        ],
        compiler_params=pltpu.CompilerParams(
            dimension_semantics=("arbitrary", "arbitrary"), vmem_limit_bytes=VMEM_LIMIT),
        name="inproj",
    )(x, mod_l, gains_l, w_in_l)


def _moba_kernel(cfar_ref, q_ref, k_ref, v_ref, km_ref, bias_ref, o_ref,
                 qt_scr, kh_scr, vt_scr, *, nb):
    hp = pl.program_id(1)
    seq = nb * MOBA_BLOCK
    lane = lax.broadcasted_iota(jnp.int32, (1, LANES), 1)
    first_head = lane < HEAD_DIM

    j_io = lax.broadcasted_iota(jnp.int32, (nb, seq), 0)
    qblk = lax.broadcasted_iota(jnp.int32, (nb, seq), 1) // MOBA_BLOCK
    valid = j_io < qblk
    is_prev = j_io == qblk - 1
    is_own = j_io == qblk
    q_all = q_ref[...]
    km = km_ref[...]
    masks = []
    for h in range(2):
        kmh = jnp.where(first_head, km, 0.0) if h == 0 else jnp.where(first_head, 0.0, km)
        p0 = kmh.astype(BF16)
        r1 = kmh - p0.astype(F32)
        p1 = r1.astype(BF16)
        p2 = (r1 - p1.astype(F32)).astype(BF16)
        lhs = jnp.concatenate([p0, p1, p2, jnp.zeros_like(p0)], axis=0)
        g4 = lax.dot_general(lhs, q_all, (((1,), (1,)), ((), ())), preferred_element_type=F32)
        gate = g4[0:nb] + g4[nb:2 * nb] + g4[2 * nb:3 * nb]
        gm = jnp.where(valid, gate, NEG_INF)
        cnt = jnp.zeros((nb, seq), jnp.int32)
        for jp in range(nb):
            other = gm[jp:jp + 1, :]
            beats = (other > gm) | ((other == gm) & (jp < j_io))
            cnt = cnt + beats.astype(jnp.int32)
        sel = valid & (jnp.abs(gate) < jnp.inf) & (cnt < MOBA_TOPK)
        cfar2 = cfar_ref[2 * hp + h] * LOG2E
        am = jnp.where(is_own | (sel & is_prev), 0.0, jnp.where(sel, cfar2, MASKED))
        am_hi = am.astype(BF16).astype(F32)
        am_lo = (am - am_hi).astype(BF16).astype(F32)
        masks.append((am_hi, am_lo))

    n_ind = 2 * nb
    pad = jnp.zeros((HEAD_DIM - n_ind, MOBA_BLOCK), F32)
    tail = jnp.where(lax.broadcasted_iota(jnp.int32, (VT_ROWS - HEAD_DIM, MOBA_BLOCK), 0) == 0, 1.0, 0.0)
    for i in range(nb):
        rows = slice(i * MOBA_BLOCK, (i + 1) * MOBA_BLOCK)
        q_t = (q_ref[rows, :].astype(F32) * (SCALE * LOG2E)).T
        qt_scr[0, i] = jnp.concatenate(
            [q_t[0:HEAD_DIM], masks[0][0][:, rows], masks[0][1][:, rows], pad], axis=0).astype(BF16)
        qt_scr[1, i] = jnp.concatenate(
            [masks[1][0][:, rows], masks[1][1][:, rows], pad, q_t[HEAD_DIM:]], axis=0).astype(BF16)
        k_i = k_ref[rows, :].astype(F32)
        ind0 = jnp.where((lane == HEAD_DIM + i) | (lane == HEAD_DIM + nb + i), 1.0, 0.0)
        ind1 = jnp.where((lane == i) | (lane == nb + i), 1.0, 0.0)
        kh_scr[0, rows, :] = jnp.where(first_head, k_i, ind0).astype(BF16)
        kh_scr[1, rows, :] = jnp.where(first_head, ind1, k_i).astype(BF16)
        v_t = v_ref[rows, :].astype(F32).T
        vt_scr[0, :, rows] = jnp.concatenate([v_t[0:HEAD_DIM], tail], axis=0).astype(BF16)
        vt_scr[1, :, rows] = jnp.concatenate([v_t[HEAD_DIM:], tail], axis=0).astype(BF16)

    def score_tile(i, h, j):
        keys = slice(j * MOBA_BLOCK, (j + 1) * MOBA_BLOCK)
        s_j = jnp.dot(kh_scr[h, keys, :], qt_scr[h, i], preferred_element_type=F32)
        if j == i:
            return s_j + bias_ref[h, 0]
        if j == i - 1:
            return s_j + bias_ref[h, 1]
        return s_j

    def finish(i, acc0, acc1):
        out_t = jnp.concatenate(
            [acc0[0:HEAD_DIM] * (1.0 / acc0[HEAD_DIM:HEAD_DIM + 1]),
             acc1[0:HEAD_DIM] * (1.0 / acc1[HEAD_DIM:HEAD_DIM + 1])], axis=0)
        o_ref[i * MOBA_BLOCK:(i + 1) * MOBA_BLOCK, :] = out_t.T.astype(BF16)

    units = [(i, h) for i in range(nb) for h in range(2)]
    accs = {}
    cur = None
    for unit in units + [None]:
        n_next = unit[0] + 1 if unit is not None else 0
        n_cur = cur[0][0] + 1 if cur is not None else 0
        next_tiles, probs = [], []
        for t in range(max(n_next, n_cur)):
            if t < n_next:
                next_tiles.append(score_tile(unit[0], unit[1], t))
            if t < n_cur:
                probs.append(jnp.exp2(cur[1][t] - cur[2]).astype(BF16))
        if cur is not None:
            ci, ch = cur[0]
            p = jnp.concatenate(probs, axis=0)
            accs[(ci, ch)] = jnp.dot(vt_scr[ch, :, 0:n_cur * MOBA_BLOCK], p, preferred_element_type=F32)
            if ch == 1:
                finish(ci, accs.pop((ci, 0)), accs.pop((ci, 1)))
        if unit is not None:
            m = functools.reduce(jnp.maximum, [jnp.max(t, axis=0, keepdims=True) for t in next_tiles])
            cur = (unit, next_tiles, m)
        else:
            cur = None


def _moba(proj, kmean, bias_moba, cfar):
    bsz, seq, _ = proj.shape
    nb = seq // MOBA_BLOCK
    n_pairs = W_A // LANES
    kern = functools.partial(_moba_kernel, nb=nb)
    return pl.pallas_call(
        kern,
        grid=(bsz, n_pairs),
        in_specs=[
            pl.BlockSpec(memory_space=pltpu.SMEM),
            pl.BlockSpec((None, seq, LANES), lambda b, hp: (b, 0, hp)),
            pl.BlockSpec((None, seq, LANES), lambda b, hp: (b, 0, n_pairs + hp)),
            pl.BlockSpec((None, seq, LANES), lambda b, hp: (b, 0, 2 * n_pairs + hp)),
            pl.BlockSpec((None, nb, LANES), lambda b, hp: (b, 0, hp)),
            pl.BlockSpec((2, 2, MOBA_BLOCK, MOBA_BLOCK), lambda b, hp: (hp, 0, 0, 0)),
        ],
        out_specs=pl.BlockSpec((None, seq, LANES), lambda b, hp: (b, 0, hp)),
        out_shape=jax.ShapeDtypeStruct((bsz, seq, W_A), BF16),
        scratch_shapes=[
            pltpu.VMEM((2, nb, LANES, MOBA_BLOCK), BF16),
            pltpu.VMEM((2, seq, LANES), BF16),
            pltpu.VMEM((2, VT_ROWS, seq), BF16),
        ],
        compiler_params=pltpu.CompilerParams(
            dimension_semantics=("arbitrary", "arbitrary"), vmem_limit_bytes=VMEM_LIMIT),
        name="moba",
    )(cfar, proj, proj, proj, kmean, bias_moba)


def _swa_kernel(sink_ref, q_ref, k_ref, v_ref, bias_ref, o_ref, qt_scr, kh_scr, vt_scr, *, nbq):
    w = SWA_WINDOW
    gw = GQA_GROUP * w
    lane = lax.broadcasted_iota(jnp.int32, (1, LANES), 1)
    first_half = lane < HEAD_DIM
    zeros_q = jnp.zeros((HEAD_DIM, gw), F32)
    tail = jnp.where(lax.broadcasted_iota(jnp.int32, (VT_ROWS - HEAD_DIM, w), 0) == 0, 1.0, 0.0)

    for n in range(nbq):
        rows = slice(n * w, (n + 1) * w)
        k_n = k_ref[rows, :].astype(F32)
        kh_scr[0, rows, :] = jnp.where(first_half, k_n, 0.0).astype(BF16)
        kh_scr[1, rows, :] = jnp.where(first_half, 0.0, k_n).astype(BF16)
        v_t = v_ref[rows, :].astype(F32).T
        vt_scr[0, :, rows] = jnp.concatenate([v_t[0:HEAD_DIM], tail], axis=0).astype(BF16)
        vt_scr[1, :, rows] = jnp.concatenate([v_t[HEAD_DIM:], tail], axis=0).astype(BF16)
        q_t = (q_ref[rows, :].astype(F32) * (SCALE * LOG2E)).T
        for g in range(N_KV_SWA):
            heads = [q_t[(g * GQA_GROUP + hh) * HEAD_DIM:(g * GQA_GROUP + hh + 1) * HEAD_DIM]
                     for hh in range(GQA_GROUP)]
            q_g = jnp.concatenate(heads, axis=1)
            parts = [q_g, zeros_q] if g == 0 else [zeros_q, q_g]
            qt_scr[g, n] = jnp.concatenate(parts, axis=0).astype(BF16)

    group_lane = lax.broadcasted_iota(jnp.int32, (1, gw), 1) // w
    sink_rows = []
    for g in range(N_KV_SWA):
        sr = jnp.zeros((1, gw), F32)
        for hh in range(GQA_GROUP):
            sr = jnp.where(group_lane == hh, sink_ref[g * GQA_GROUP + hh] * LOG2E, sr)
        sink_rows.append(sr)

    def scores(n, g):
        k0 = max(n - 1, 0) * w
        s = jnp.dot(kh_scr[g, k0:(n + 1) * w, :], qt_scr[g, n], preferred_element_type=F32)
        s = s + (bias_ref[g] if n > 0 else bias_ref[g, w:2 * w, :])
        m = jnp.maximum(jnp.max(s, axis=0, keepdims=True), sink_rows[g])
        return s, m

    def finish(n, g, s, m):
        k0 = max(n - 1, 0) * w
        p = jnp.exp2(s - m).astype(BF16)
        acc = jnp.dot(vt_scr[g, :, k0:(n + 1) * w], p, preferred_element_type=F32)
        l = acc[HEAD_DIM:HEAD_DIM + 1] + jnp.exp2(sink_rows[g] - m)
        o = acc[0:HEAD_DIM] * (1.0 / l)
        for pp in range(GQA_GROUP // 2):
            pair_t = jnp.concatenate([o[:, (2 * pp) * w:(2 * pp + 1) * w],
                                      o[:, (2 * pp + 1) * w:(2 * pp + 2) * w]], axis=0)
            col = (g * (GQA_GROUP // 2) + pp) * LANES
            o_ref[n * w:(n + 1) * w, col:col + LANES] = pair_t.T.astype(BF16)

    units = [(n, g) for n in range(nbq) for g in range(N_KV_SWA)]
    cur = None
    for unit in units + [None]:
        nxt = (unit, scores(*unit)) if unit is not None else None
        if cur is not None:
            finish(*cur[0], *cur[1])
        cur = nxt


def _swa(proj, bias_swa, sinks_l):
    bsz, seq, _ = proj.shape
    nbq = seq // SWA_WINDOW
    kern = functools.partial(_swa_kernel, nbq=nbq)
    q_col = (3 * W_A) // W_QB
    k_col = (3 * W_A + W_QB) // W_KVB
    return pl.pallas_call(
        kern,
        grid=(bsz,),
        in_specs=[
            pl.BlockSpec(memory_space=pltpu.SMEM),
            pl.BlockSpec((None, seq, W_QB), lambda b: (b, 0, q_col)),
            pl.BlockSpec((None, seq, W_KVB), lambda b: (b, 0, k_col)),
            pl.BlockSpec((None, seq, W_KVB), lambda b: (b, 0, k_col + 1)),
            pl.BlockSpec((N_KV_SWA, 2 * SWA_WINDOW, GQA_GROUP * SWA_WINDOW), lambda b: (0, 0, 0)),
        ],
        out_specs=pl.BlockSpec((None, seq, W_QB), lambda b: (b, 0, 0)),
        out_shape=jax.ShapeDtypeStruct((bsz, seq, W_QB), BF16),
        scratch_shapes=[
            pltpu.VMEM((N_KV_SWA, nbq, LANES, GQA_GROUP * SWA_WINDOW), BF16),
            pltpu.VMEM((N_KV_SWA, seq, LANES), BF16),
            pltpu.VMEM((N_KV_SWA, VT_ROWS, seq), BF16),
        ],
        compiler_params=pltpu.CompilerParams(
            dimension_semantics=("arbitrary",), vmem_limit_bytes=VMEM_LIMIT),
        name="swa",
    )(sinks_l, proj, proj, proj, bias_swa)


def _mix_mlp_kernel(a_ref, b_ref, x_ref, mod_ref, g_ref, wo_ref, w1_ref, b1_ref, w2_ref, b2_ref, o_ref):
    y = (jnp.dot(a_ref[...], wo_ref[0:W_A, :], preferred_element_type=F32)
         + jnp.dot(b_ref[...], wo_ref[W_A:, :], preferred_element_type=F32))
    x = x_ref[...] + mod_ref[2:3, :] * _rms(y, g_ref[1:2, :])
    h = _rms(x, g_ref[2:3, :]) * (1.0 + mod_ref[4:5, :]) + mod_ref[3:4, :]
    hb = h.astype(BF16)
    acc = jnp.zeros(x.shape, F32)
    for c0 in range(0, D_FF, FF_CHUNK):
        u = jnp.dot(hb, w1_ref[:, c0:c0 + FF_CHUNK], preferred_element_type=F32) + b1_ref[:, c0:c0 + FF_CHUNK]
        u = jnp.square(jnp.maximum(u, 0.0))
        acc = acc + jnp.dot(u.astype(BF16), w2_ref[c0:c0 + FF_CHUNK, :], preferred_element_type=F32)
    y2 = acc + b2_ref[...]
    o_ref[...] = x + mod_ref[5:6, :] * _rms(y2, g_ref[3:4, :])


def _mix_mlp(attn_a, attn_b, x, mod_l, gains_l, w_out_l, w1_l, b1_l, w2_l, b2_l):
    bsz, seq, d = x.shape
    const = lambda shape: pl.BlockSpec(shape, lambda b, t: (0,) * len(shape), pipeline_mode=pl.Buffered(1))
    return pl.pallas_call(
        _mix_mlp_kernel,
        grid=(bsz, seq // ROW_TILE),
        in_specs=[
            pl.BlockSpec((None, ROW_TILE, W_A), lambda b, t: (b, t, 0)),
            pl.BlockSpec((None, ROW_TILE, W_QB), lambda b, t: (b, t, 0)),
            pl.BlockSpec((None, ROW_TILE, d), lambda b, t: (b, t, 0)),
            pl.BlockSpec((None, 6, d), lambda b, t: (b, 0, 0)),
            pl.BlockSpec((4, d), lambda b, t: (0, 0)),
            const((d, d)),
            const((d, D_FF)),
            const((1, D_FF)),
            const((D_FF, d)),
            const((1, d)),
        ],
        out_specs=pl.BlockSpec((None, ROW_TILE, d), lambda b, t: (b, t, 0)),
        out_shape=jax.ShapeDtypeStruct((bsz, seq, d), F32),
        compiler_params=pltpu.CompilerParams(
            dimension_semantics=("arbitrary", "arbitrary"), vmem_limit_bytes=VMEM_LIMIT),
        name="mix_mlp",
    )(attn_a, attn_b, x, mod_l, gains_l, w_out_l, w1_l, b1_l, w2_l, b2_l)


def kernel(x, c, w_in, w_out, sinks, rel_bias, w_ada, b_ada, norm_gains, w1, b1, w2, b2):
    bsz, seq, d = x.shape
    depth = w_in.shape[0]
    assert d == D_MODEL and seq % ROW_TILE == 0 and ROW_TILE % MOBA_BLOCK == 0
    assert seq // MOBA_BLOCK == 8, "the block-gate matmul packs 8 block means per bf16 tile row group"

    mod = _adaln(c, w_ada, b_ada).reshape(depth, bsz, 6, d)
    bias_moba, bias_swa = _bias_tiles(rel_bias)
    cfar = rel_bias[N_BUCKETS - 1, :N_HEADS_MOBA]

    w_in_b = w_in.astype(BF16)
    w_out_b = w_out.astype(BF16)
    w1_b = w1.astype(BF16)
    w2_b = w2.astype(BF16)

    for l in range(depth):
        proj, kmean = _inproj(x, mod[l], norm_gains[l], w_in_b[l])
        attn_a = _moba(proj, kmean, bias_moba, cfar)
        attn_b = _swa(proj, bias_swa, sinks[l])
        x = _mix_mlp(attn_a, attn_b, x, mod[l], norm_gains[l], w_out_b[l], w1_b[l],
                     b1[l].reshape(1, D_FF), w2_b[l], b2[l].reshape(1, d))
    return x
```

```python
import functools
import math

import jax
import jax.numpy as jnp
from jax import lax
from jax.experimental import pallas as pl
from jax.experimental.pallas import tpu as pltpu

D_MODEL = 1024
HEAD_DIM = 64
N_HEADS_MOBA = 8
N_HEADS_SWA = 8
N_KV_SWA = 2
GQA_GROUP = N_HEADS_SWA // N_KV_SWA
MOBA_BLOCK = 256
MOBA_TOPK = 3
SWA_WINDOW = 128
N_BUCKETS = 32
MAX_DISTANCE = 128
D_FF = 4 * D_MODEL
EPS = 1e-6
W_A = N_HEADS_MOBA * HEAD_DIM
W_QB = N_HEADS_SWA * HEAD_DIM
W_KVB = N_KV_SWA * HEAD_DIM
D_IN = 3 * W_A + W_QB + 2 * W_KVB
SCALE = HEAD_DIM ** -0.5

LANES = 128
VMEM_LIMIT = 56 * 1024 * 1024
ROW_TILE = 512
FF_CHUNK = 1024
ADA_COL_TILE = 2048

F32 = jnp.float32
BF16 = jnp.bfloat16
NEG_INF = float("-inf")
LOG2E = 1.4426950408889634
MASKED = -1e30
VT_ROWS = 80


def _rms(x, g):
    return x * lax.rsqrt(jnp.mean(x * x, axis=-1, keepdims=True) + EPS) * g


def _t5_bucket(dist):
    n = jnp.maximum(dist, 0)
    max_exact = N_BUCKETS // 2
    nf = jnp.maximum(n, 1).astype(F32)
    large = max_exact + (jnp.log(nf / max_exact) / math.log(MAX_DISTANCE / max_exact)
                         * (N_BUCKETS - max_exact)).astype(jnp.int32)
    large = jnp.minimum(large, N_BUCKETS - 1)
    return jnp.where(n < max_exact, n, large)


def _bias_from_dist(dist, rb_ref, head):
    bucket = _t5_bucket(dist)
    val = jnp.zeros(dist.shape, F32)
    for b in range(N_BUCKETS):
        val = jnp.where(bucket == b, rb_ref[b, head], val)
    return val


def _adaln_kernel(c_ref, w_ref, b_ref, o_ref):
    c = c_ref[...]
    c_act = c * (1.0 / (1.0 + jnp.exp(-c)))
    o_ref[...] = jnp.dot(c_act.astype(BF16), w_ref[...].astype(BF16),
                         preferred_element_type=F32) + b_ref[...]


def _adaln(c, w_ada, b_ada):
    depth, d, n = w_ada.shape
    bsz = c.shape[0]
    return pl.pallas_call(
        _adaln_kernel,
        grid=(depth, n // ADA_COL_TILE),
        in_specs=[
            pl.BlockSpec((bsz, d), lambda l, j: (0, 0)),
            pl.BlockSpec((None, d, ADA_COL_TILE), lambda l, j: (l, 0, j)),
            pl.BlockSpec((None, 1, ADA_COL_TILE), lambda l, j: (l, 0, j)),
        ],
        out_specs=pl.BlockSpec((None, bsz, ADA_COL_TILE), lambda l, j: (l, 0, j)),
        out_shape=jax.ShapeDtypeStruct((depth, bsz, n), F32),
        compiler_params=pltpu.CompilerParams(
            dimension_semantics=("arbitrary", "arbitrary"), vmem_limit_bytes=VMEM_LIMIT),
        name="adaln",
    )(c, w_ada, b_ada.reshape(depth, 1, n))


def _moba_bias_kernel(rb_ref, o_ref):
    h = pl.program_id(0)
    kk = lax.broadcasted_iota(jnp.int32, (MOBA_BLOCK, MOBA_BLOCK), 0)
    qq = lax.broadcasted_iota(jnp.int32, (MOBA_BLOCK, MOBA_BLOCK), 1)
    d_own = qq - kk
    o_ref[0] = jnp.where(d_own >= 0, _bias_from_dist(d_own, rb_ref, h) * LOG2E, NEG_INF)
    o_ref[1] = _bias_from_dist(d_own + MOBA_BLOCK, rb_ref, h) * LOG2E


def _swa_bias_kernel(rb_ref, o_ref):
    h = pl.program_id(0) + N_HEADS_MOBA
    jj = lax.broadcasted_iota(jnp.int32, (2 * SWA_WINDOW, SWA_WINDOW), 0)
    qq = lax.broadcasted_iota(jnp.int32, (2 * SWA_WINDOW, SWA_WINDOW), 1)
    dist = qq + SWA_WINDOW - jj
    allowed = (dist >= 0) & (dist < SWA_WINDOW)
    o_ref[...] = jnp.where(allowed, _bias_from_dist(dist, rb_ref, h) * LOG2E, NEG_INF)


def _bias_tiles(rel_bias):
    smem = pl.BlockSpec(memory_space=pltpu.SMEM)
    moba = pl.pallas_call(
        _moba_bias_kernel,
        grid=(N_HEADS_MOBA,),
        in_specs=[smem],
        out_specs=pl.BlockSpec((None, 2, MOBA_BLOCK, MOBA_BLOCK), lambda h: (h, 0, 0, 0)),
        out_shape=jax.ShapeDtypeStruct((N_HEADS_MOBA, 2, MOBA_BLOCK, MOBA_BLOCK), F32),
        compiler_params=pltpu.CompilerParams(dimension_semantics=("arbitrary",)),
        name="moba_bias",
    )(rel_bias)
    swa = pl.pallas_call(
        _swa_bias_kernel,
        grid=(N_HEADS_SWA,),
        in_specs=[smem],
        out_specs=pl.BlockSpec((None, 2 * SWA_WINDOW, SWA_WINDOW),
                               lambda h: (h // GQA_GROUP, 0, h % GQA_GROUP)),
        out_shape=jax.ShapeDtypeStruct((N_KV_SWA, 2 * SWA_WINDOW, GQA_GROUP * SWA_WINDOW), F32),
        compiler_params=pltpu.CompilerParams(dimension_semantics=("arbitrary",)),
        name="swa_bias",
    )(rel_bias)
    return moba, swa


def _inproj_kernel(x_ref, mod_ref, g_ref, w_ref, proj_ref, kmean_ref):
    t = pl.program_id(1)
    h = _rms(x_ref[...], g_ref[0:1, :]) * (1.0 + mod_ref[1:2, :]) + mod_ref[0:1, :]
    hb = h.astype(BF16)
    blocks_per_tile = ROW_TILE // MOBA_BLOCK
    for c0 in range(0, D_IN, W_A):
        c1 = min(c0 + W_A, D_IN)
        p = jnp.dot(hb, w_ref[:, c0:c1], preferred_element_type=F32)
        proj_ref[:, c0:c1] = p.astype(BF16)
        if c0 == W_A:
            for r in range(blocks_per_tile):
                kmean_ref[pl.ds(t * blocks_per_tile + r, 1), :] = jnp.mean(
                    p[r * MOBA_BLOCK:(r + 1) * MOBA_BLOCK], axis=0, keepdims=True)


def _inproj(x, mod, gains, w_in, l):
    bsz, seq, d = x.shape
    nb = seq // MOBA_BLOCK
    return pl.pallas_call(
        _inproj_kernel,
        grid=(bsz, seq // ROW_TILE),
        in_specs=[
            pl.BlockSpec((None, ROW_TILE, d), lambda b, t: (b, t, 0)),
            pl.BlockSpec((None, None, 6, d), lambda b, t: (l, b, 0, 0)),
            pl.BlockSpec((None, 4, d), lambda b, t: (l, 0, 0)),
            pl.BlockSpec((None, d, D_IN), lambda b, t: (l, 0, 0)),
        ],
        out_specs=[
            pl.BlockSpec((None, ROW_TILE, D_IN), lambda b, t: (b, t, 0)),
            pl.BlockSpec((None, nb, W_A), lambda b, t: (b, 0, 0)),
        ],
        out_shape=[
            jax.ShapeDtypeStruct((bsz, seq, D_IN), BF16),
            jax.ShapeDtypeStruct((bsz, nb, W_A), F32),
        ],
        compiler_params=pltpu.CompilerParams(
            dimension_semantics=("arbitrary", "arbitrary"), vmem_limit_bytes=VMEM_LIMIT),
        name="inproj",
    )(x, mod, gains, w_in)


def _moba_kernel(cfar_ref, q_ref, k_ref, v_ref, km_ref, bias_ref, o_ref,
                 qt_scr, kh_scr, vt_scr, *, nb):
    hp = pl.program_id(1)
    seq = nb * MOBA_BLOCK
    lane = lax.broadcasted_iota(jnp.int32, (1, LANES), 1)
    first_head = lane < HEAD_DIM

    j_io = lax.broadcasted_iota(jnp.int32, (nb, seq), 0)
    qblk = lax.broadcasted_iota(jnp.int32, (nb, seq), 1) // MOBA_BLOCK
    valid = j_io < qblk
    is_prev = j_io == qblk - 1
    is_own = j_io == qblk
    q_all = q_ref[...]
    km = km_ref[...]
    masks = []
    for h in range(2):
        kmh = jnp.where(first_head, km, 0.0) if h == 0 else jnp.where(first_head, 0.0, km)
        p0 = kmh.astype(BF16)
        r1 = kmh - p0.astype(F32)
        p1 = r1.astype(BF16)
        p2 = (r1 - p1.astype(F32)).astype(BF16)
        lhs = jnp.concatenate([p0, p1, p2, jnp.zeros_like(p0)], axis=0)
        g4 = lax.dot_general(lhs, q_all, (((1,), (1,)), ((), ())), preferred_element_type=F32)
        gate = g4[0:nb] + g4[nb:2 * nb] + g4[2 * nb:3 * nb]
        gm = jnp.where(valid, gate, NEG_INF)
        cnt = jnp.zeros((nb, seq), jnp.int32)
        for jp in range(nb):
            other = gm[jp:jp + 1, :]
            beats = (other > gm) | ((other == gm) & (jp < j_io))
            cnt = cnt + beats.astype(jnp.int32)
        sel = valid & (jnp.abs(gate) < jnp.inf) & (cnt < MOBA_TOPK)
        cfar2 = cfar_ref[2 * hp + h] * LOG2E
        am = jnp.where(is_own | (sel & is_prev), 0.0, jnp.where(sel, cfar2, MASKED))
        am_hi = am.astype(BF16).astype(F32)
        am_lo = (am - am_hi).astype(BF16).astype(F32)
        masks.append((am_hi, am_lo))

    n_ind = 2 * nb
    pad = jnp.zeros((HEAD_DIM - n_ind, MOBA_BLOCK), F32)
    tail = jnp.where(lax.broadcasted_iota(jnp.int32, (VT_ROWS - HEAD_DIM, MOBA_BLOCK), 0) == 0, 1.0, 0.0)
    for i in range(nb):
        rows = slice(i * MOBA_BLOCK, (i + 1) * MOBA_BLOCK)
        q_t = (q_ref[rows, :].astype(F32) * (SCALE * LOG2E)).T
        cols = slice((i % 2) * MOBA_BLOCK, (i % 2 + 1) * MOBA_BLOCK)
        qt_scr[0, i // 2, :, cols] = jnp.concatenate(
            [q_t[0:HEAD_DIM], masks[0][0][:, rows], masks[0][1][:, rows], pad], axis=0).astype(BF16)
        qt_scr[1, i // 2, :, cols] = jnp.concatenate(
            [masks[1][0][:, rows], masks[1][1][:, rows], pad, q_t[HEAD_DIM:]], axis=0).astype(BF16)
        k_i = k_ref[rows, :].astype(F32)
        ind0 = jnp.where((lane == HEAD_DIM + i) | (lane == HEAD_DIM + nb + i), 1.0, 0.0)
        ind1 = jnp.where((lane == i) | (lane == nb + i), 1.0, 0.0)
        kh_scr[0, rows, :] = jnp.where(first_head, k_i, ind0).astype(BF16)
        kh_scr[1, rows, :] = jnp.where(first_head, ind1, k_i).astype(BF16)
        v_t = v_ref[rows, :].astype(F32).T
        vt_scr[0, :, rows] = jnp.concatenate([v_t[0:HEAD_DIM], tail], axis=0).astype(BF16)
        vt_scr[1, :, rows] = jnp.concatenate([v_t[HEAD_DIM:], tail], axis=0).astype(BF16)

    blk = MOBA_BLOCK

    def pair_scores(ip, h):
        i = 2 * ip
        s = jnp.dot(kh_scr[h, 0:(i + 1) * blk, :], qt_scr[h, ip], preferred_element_type=F32)
        tiles = []
        for j in range(i + 1):
            s_j = s[j * blk:(j + 1) * blk]
            if j == i:
                s_j = s_j + jnp.concatenate([bias_ref[h, 0], bias_ref[h, 1]], axis=1)
            elif j == i - 1:
                s_j = jnp.concatenate([s_j[:, 0:blk] + bias_ref[h, 1], s_j[:, blk:]], axis=1)
            tiles.append(s_j)
        last = jnp.dot(kh_scr[h, (i + 1) * blk:(i + 2) * blk, :], qt_scr[h, ip, :, blk:2 * blk],
                       preferred_element_type=F32) + bias_ref[h, 0]
        m = functools.reduce(jnp.maximum, [jnp.max(t, axis=0, keepdims=True) for t in tiles])
        m = jnp.concatenate(
            [m[:, 0:blk], jnp.maximum(m[:, blk:], jnp.max(last, axis=0, keepdims=True))], axis=1)
        return tiles, last, m

    def pair_values(ip, h, tiles, last, m):
        i = 2 * ip
        p = jnp.concatenate([jnp.exp2(t - m).astype(BF16) for t in tiles], axis=0)
        acc = jnp.dot(vt_scr[h, :, 0:(i + 1) * blk], p, preferred_element_type=F32)
        p_last = jnp.exp2(last - m[:, blk:]).astype(BF16)
        acc_last = jnp.dot(vt_scr[h, :, (i + 1) * blk:(i + 2) * blk], p_last, preferred_element_type=F32)
        return acc[:, 0:blk], acc[:, blk:] + acc_last

    def finish(i, acc0, acc1):
        out_t = jnp.concatenate(
            [acc0[0:HEAD_DIM] * (1.0 / acc0[HEAD_DIM:HEAD_DIM + 1]),
             acc1[0:HEAD_DIM] * (1.0 / acc1[HEAD_DIM:HEAD_DIM + 1])], axis=0)
        o_ref[i * blk:(i + 1) * blk, :] = out_t.T.astype(BF16)

    units = [(ip, h) for ip in range(nb // 2) for h in range(2)]
    accs = {}
    cur = None
    for unit in units + [None]:
        nxt = (unit, pair_scores(*unit)) if unit is not None else None
        if cur is not None:
            (ip, h), st = cur
            accs[h] = pair_values(ip, h, *st)
            if h == 1:
                finish(2 * ip, accs[0][0], accs[1][0])
                finish(2 * ip + 1, accs[0][1], accs[1][1])
        cur = nxt


def _moba(proj, kmean, bias_moba, cfar):
    bsz, seq, _ = proj.shape
    nb = seq // MOBA_BLOCK
    n_pairs = W_A // LANES
    kern = functools.partial(_moba_kernel, nb=nb)
    return pl.pallas_call(
        kern,
        grid=(bsz, n_pairs),
        in_specs=[
            pl.BlockSpec(memory_space=pltpu.SMEM),
            pl.BlockSpec((None, seq, LANES), lambda b, hp: (b, 0, hp)),
            pl.BlockSpec((None, seq, LANES), lambda b, hp: (b, 0, n_pairs + hp)),
            pl.BlockSpec((None, seq, LANES), lambda b, hp: (b, 0, 2 * n_pairs + hp)),
            pl.BlockSpec((None, nb, LANES), lambda b, hp: (b, 0, hp)),
            pl.BlockSpec((2, 2, MOBA_BLOCK, MOBA_BLOCK), lambda b, hp: (hp, 0, 0, 0)),
        ],
        out_specs=pl.BlockSpec((None, seq, LANES), lambda b, hp: (b, 0, hp)),
        out_shape=jax.ShapeDtypeStruct((bsz, seq, W_A), BF16),
        scratch_shapes=[
            pltpu.VMEM((2, nb // 2, LANES, 2 * MOBA_BLOCK), BF16),
            pltpu.VMEM((2, seq, LANES), BF16),
            pltpu.VMEM((2, VT_ROWS, seq), BF16),
        ],
        compiler_params=pltpu.CompilerParams(
            dimension_semantics=("arbitrary", "arbitrary"), vmem_limit_bytes=VMEM_LIMIT),
        name="moba",
    )(cfar, proj, proj, proj, kmean, bias_moba)


def _swa_kernel(sink_ref, q_ref, k_ref, v_ref, bias_ref, o_ref, qt_scr, kh_scr, vt_scr, *, nbq, layer):
    w = SWA_WINDOW
    gw = GQA_GROUP * w
    lane = lax.broadcasted_iota(jnp.int32, (1, LANES), 1)
    first_half = lane < HEAD_DIM
    zeros_q = jnp.zeros((HEAD_DIM, gw), F32)
    tail = jnp.where(lax.broadcasted_iota(jnp.int32, (VT_ROWS - HEAD_DIM, w), 0) == 0, 1.0, 0.0)

    for n in range(nbq):
        rows = slice(n * w, (n + 1) * w)
        k_n = k_ref[rows, :].astype(F32)
        kh_scr[0, rows, :] = jnp.where(first_half, k_n, 0.0).astype(BF16)
        kh_scr[1, rows, :] = jnp.where(first_half, 0.0, k_n).astype(BF16)
        v_t = v_ref[rows, :].astype(F32).T
        vt_scr[0, :, rows] = jnp.concatenate([v_t[0:HEAD_DIM], tail], axis=0).astype(BF16)
        vt_scr[1, :, rows] = jnp.concatenate([v_t[HEAD_DIM:], tail], axis=0).astype(BF16)
        q_t = (q_ref[rows, :].astype(F32) * (SCALE * LOG2E)).T
        for g in range(N_KV_SWA):
            heads = [q_t[(g * GQA_GROUP + hh) * HEAD_DIM:(g * GQA_GROUP + hh + 1) * HEAD_DIM]
                     for hh in range(GQA_GROUP)]
            q_g = jnp.concatenate(heads, axis=1)
            parts = [q_g, zeros_q] if g == 0 else [zeros_q, q_g]
            qt_scr[g, n] = jnp.concatenate(parts, axis=0).astype(BF16)

    group_lane = lax.broadcasted_iota(jnp.int32, (1, gw), 1) // w
    sink_rows = []
    for g in range(N_KV_SWA):
        sr = jnp.zeros((1, gw), F32)
        for hh in range(GQA_GROUP):
            sr = jnp.where(group_lane == hh, sink_ref[layer, g * GQA_GROUP + hh] * LOG2E, sr)
        sink_rows.append(sr)

    def scores(n, g):
        k0 = max(n - 1, 0) * w
        s = jnp.dot(kh_scr[g, k0:(n + 1) * w, :], qt_scr[g, n], preferred_element_type=F32)
        s = s + (bias_ref[g] if n > 0 else bias_ref[g, w:2 * w, :])
        m = jnp.maximum(jnp.max(s, axis=0, keepdims=True), sink_rows[g])
        return s, m

    def finish(n, g, s, m):
        k0 = max(n - 1, 0) * w
        p = jnp.exp2(s - m).astype(BF16)
        acc = jnp.dot(vt_scr[g, :, k0:(n + 1) * w], p, preferred_element_type=F32)
        l = acc[HEAD_DIM:HEAD_DIM + 1] + jnp.exp2(sink_rows[g] - m)
        o = acc[0:HEAD_DIM] * (1.0 / l)
        for pp in range(GQA_GROUP // 2):
            pair_t = jnp.concatenate([o[:, (2 * pp) * w:(2 * pp + 1) * w],
                                      o[:, (2 * pp + 1) * w:(2 * pp + 2) * w]], axis=0)
            col = (g * (GQA_GROUP // 2) + pp) * LANES
            o_ref[n * w:(n + 1) * w, col:col + LANES] = pair_t.T.astype(BF16)

    units = [(n, g) for n in range(nbq) for g in range(N_KV_SWA)]
    cur = None
    for unit in units + [None]:
        nxt = (unit, scores(*unit)) if unit is not None else None
        if cur is not None:
            finish(*cur[0], *cur[1])
        cur = nxt


def _swa(proj, bias_swa, sinks, l):
    bsz, seq, _ = proj.shape
    nbq = seq // SWA_WINDOW
    kern = functools.partial(_swa_kernel, nbq=nbq, layer=l)
    q_col = (3 * W_A) // W_QB
    k_col = (3 * W_A + W_QB) // W_KVB
    return pl.pallas_call(
        kern,
        grid=(bsz,),
        in_specs=[
            pl.BlockSpec(memory_space=pltpu.SMEM),
            pl.BlockSpec((None, seq, W_QB), lambda b: (b, 0, q_col)),
            pl.BlockSpec((None, seq, W_KVB), lambda b: (b, 0, k_col)),
            pl.BlockSpec((None, seq, W_KVB), lambda b: (b, 0, k_col + 1)),
            pl.BlockSpec((N_KV_SWA, 2 * SWA_WINDOW, GQA_GROUP * SWA_WINDOW), lambda b: (0, 0, 0)),
        ],
        out_specs=pl.BlockSpec((None, seq, W_QB), lambda b: (b, 0, 0)),
        out_shape=jax.ShapeDtypeStruct((bsz, seq, W_QB), BF16),
        scratch_shapes=[
            pltpu.VMEM((N_KV_SWA, nbq, LANES, GQA_GROUP * SWA_WINDOW), BF16),
            pltpu.VMEM((N_KV_SWA, seq, LANES), BF16),
            pltpu.VMEM((N_KV_SWA, VT_ROWS, seq), BF16),
        ],
        compiler_params=pltpu.CompilerParams(
            dimension_semantics=("arbitrary",), vmem_limit_bytes=VMEM_LIMIT),
        name="swa",
    )(sinks, proj, proj, proj, bias_swa)


def _mix_mlp_kernel(a_ref, b_ref, x_ref, mod_ref, g_ref, wo_ref, w1_ref, b1_ref, w2_ref, b2_ref, o_ref):
    y = (jnp.dot(a_ref[...], wo_ref[0:W_A, :], preferred_element_type=F32)
         + jnp.dot(b_ref[...], wo_ref[W_A:, :], preferred_element_type=F32))
    x = x_ref[...] + mod_ref[2:3, :] * _rms(y, g_ref[1:2, :])
    h = _rms(x, g_ref[2:3, :]) * (1.0 + mod_ref[4:5, :]) + mod_ref[3:4, :]
    hb = h.astype(BF16)
    acc = jnp.zeros(x.shape, F32)
    for c0 in range(0, D_FF, FF_CHUNK):
        u = jnp.dot(hb, w1_ref[:, c0:c0 + FF_CHUNK], preferred_element_type=F32) + b1_ref[:, c0:c0 + FF_CHUNK]
        u = jnp.square(jnp.maximum(u, 0.0))
        acc = acc + jnp.dot(u.astype(BF16), w2_ref[c0:c0 + FF_CHUNK, :], preferred_element_type=F32)
    y2 = acc + b2_ref[...]
    o_ref[...] = x + mod_ref[5:6, :] * _rms(y2, g_ref[3:4, :])


def _mix_mlp(attn_a, attn_b, x, mod, gains, w_out, w1, b1, w2, b2, l):
    bsz, seq, d = x.shape
    const = lambda shape: pl.BlockSpec((None,) + shape, lambda b, t: (l,) + (0,) * len(shape),
                                       pipeline_mode=pl.Buffered(1))
    return pl.pallas_call(
        _mix_mlp_kernel,
        grid=(bsz, seq // ROW_TILE),
        in_specs=[
            pl.BlockSpec((None, ROW_TILE, W_A), lambda b, t: (b, t, 0)),
            pl.BlockSpec((None, ROW_TILE, W_QB), lambda b, t: (b, t, 0)),
            pl.BlockSpec((None, ROW_TILE, d), lambda b, t: (b, t, 0)),
            pl.BlockSpec((None, None, 6, d), lambda b, t: (l, b, 0, 0)),
            pl.BlockSpec((None, 4, d), lambda b, t: (l, 0, 0)),
            const((d, d)),
            const((d, D_FF)),
            const((1, D_FF)),
            const((D_FF, d)),
            const((1, d)),
        ],
        out_specs=pl.BlockSpec((None, ROW_TILE, d), lambda b, t: (b, t, 0)),
        out_shape=jax.ShapeDtypeStruct((bsz, seq, d), F32),
        compiler_params=pltpu.CompilerParams(
            dimension_semantics=("arbitrary", "arbitrary"), vmem_limit_bytes=VMEM_LIMIT),
        name="mix_mlp",
    )(attn_a, attn_b, x, mod, gains, w_out, w1, b1, w2, b2)


def kernel(x, c, w_in, w_out, sinks, rel_bias, w_ada, b_ada, norm_gains, w1, b1, w2, b2):
    bsz, seq, d = x.shape
    depth = w_in.shape[0]
    assert d == D_MODEL and seq % ROW_TILE == 0 and ROW_TILE % MOBA_BLOCK == 0
    assert seq // MOBA_BLOCK == 8, "the block-gate matmul packs 8 block means per bf16 tile row group"

    mod = _adaln(c, w_ada, b_ada).reshape(depth, bsz, 6, d)
    bias_moba, bias_swa = _bias_tiles(rel_bias)
    cfar = rel_bias[N_BUCKETS - 1, :N_HEADS_MOBA]

    w_in_b = w_in.astype(BF16)
    w_out_b = w_out.astype(BF16)
    w1_b = w1.astype(BF16)
    w2_b = w2.astype(BF16)

    b1_r = b1.reshape(depth, 1, D_FF)
    b2_r = b2.reshape(depth, 1, d)

    for l in range(depth):
        proj, kmean = _inproj(x, mod, norm_gains, w_in_b, l)
        attn_a = _moba(proj, kmean, bias_moba, cfar)
        attn_b = _swa(proj, bias_swa, sinks, l)
        x = _mix_mlp(attn_a, attn_b, x, mod, norm_gains, w_out_b, w1_b, b1_r, w2_b, b2_r, l)
    return x
```

```python
import functools
import math

import jax
import jax.numpy as jnp
from jax import lax
from jax.experimental import pallas as pl
from jax.experimental.pallas import tpu as pltpu

D_MODEL = 1024
HEAD_DIM = 64
N_HEADS_MOBA = 8
N_HEADS_SWA = 8
N_KV_SWA = 2
GQA_GROUP = N_HEADS_SWA // N_KV_SWA
MOBA_BLOCK = 256
MOBA_TOPK = 3
SWA_WINDOW = 128
N_BUCKETS = 32
MAX_DISTANCE = 128
D_FF = 4 * D_MODEL
EPS = 1e-6
W_A = N_HEADS_MOBA * HEAD_DIM
W_QB = N_HEADS_SWA * HEAD_DIM
W_KVB = N_KV_SWA * HEAD_DIM
D_IN = 3 * W_A + W_QB + 2 * W_KVB
SCALE = HEAD_DIM ** -0.5

LANES = 128
VMEM_LIMIT = 56 * 1024 * 1024
ROW_TILE = 512
FF_CHUNK = 1024
ADA_COL_TILE = 2048

F32 = jnp.float32
BF16 = jnp.bfloat16
NEG_INF = float("-inf")
LOG2E = 1.4426950408889634
MASKED = -1e30
VT_ROWS = 80


def _rms(x, g):
    return x * lax.rsqrt(jnp.mean(x * x, axis=-1, keepdims=True) + EPS) * g


def _t5_bucket(dist):
    n = jnp.maximum(dist, 0)
    max_exact = N_BUCKETS // 2
    nf = jnp.maximum(n, 1).astype(F32)
    large = max_exact + (jnp.log(nf / max_exact) / math.log(MAX_DISTANCE / max_exact)
                         * (N_BUCKETS - max_exact)).astype(jnp.int32)
    large = jnp.minimum(large, N_BUCKETS - 1)
    return jnp.where(n < max_exact, n, large)


def _bias_from_dist(dist, rb_ref, head):
    bucket = _t5_bucket(dist)
    val = jnp.zeros(dist.shape, F32)
    for b in range(N_BUCKETS):
        val = jnp.where(bucket == b, rb_ref[b, head], val)
    return val


def _adaln_kernel(c_ref, w_ref, b_ref, o_ref):
    c = c_ref[...]
    c_act = c * (1.0 / (1.0 + jnp.exp(-c)))
    o_ref[...] = jnp.dot(c_act.astype(BF16), w_ref[...].astype(BF16),
                         preferred_element_type=F32) + b_ref[...]


def _adaln(c, w_ada, b_ada):
    depth, d, n = w_ada.shape
    bsz = c.shape[0]
    return pl.pallas_call(
        _adaln_kernel,
        grid=(depth, n // ADA_COL_TILE),
        in_specs=[
            pl.BlockSpec((bsz, d), lambda l, j: (0, 0)),
            pl.BlockSpec((None, d, ADA_COL_TILE), lambda l, j: (l, 0, j)),
            pl.BlockSpec((None, 1, ADA_COL_TILE), lambda l, j: (l, 0, j)),
        ],
        out_specs=pl.BlockSpec((None, bsz, ADA_COL_TILE), lambda l, j: (l, 0, j)),
        out_shape=jax.ShapeDtypeStruct((depth, bsz, n), F32),
        compiler_params=pltpu.CompilerParams(
            dimension_semantics=("arbitrary", "arbitrary"), vmem_limit_bytes=VMEM_LIMIT),
        name="adaln",
    )(c, w_ada, b_ada.reshape(depth, 1, n))


def _moba_bias_kernel(rb_ref, o_ref):
    h = pl.program_id(0)
    kk = lax.broadcasted_iota(jnp.int32, (MOBA_BLOCK, MOBA_BLOCK), 0)
    qq = lax.broadcasted_iota(jnp.int32, (MOBA_BLOCK, MOBA_BLOCK), 1)
    d_own = qq - kk
    o_ref[0] = jnp.where(d_own >= 0, _bias_from_dist(d_own, rb_ref, h) * LOG2E, NEG_INF)
    o_ref[1] = _bias_from_dist(d_own + MOBA_BLOCK, rb_ref, h) * LOG2E


def _swa_bias_kernel(rb_ref, o_ref):
    h = pl.program_id(0) + N_HEADS_MOBA
    jj = lax.broadcasted_iota(jnp.int32, (2 * SWA_WINDOW, SWA_WINDOW), 0)
    qq = lax.broadcasted_iota(jnp.int32, (2 * SWA_WINDOW, SWA_WINDOW), 1)
    dist = qq + SWA_WINDOW - jj
    allowed = (dist >= 0) & (dist < SWA_WINDOW)
    o_ref[...] = jnp.where(allowed, _bias_from_dist(dist, rb_ref, h) * LOG2E, NEG_INF)


def _bias_tiles(rel_bias):
    smem = pl.BlockSpec(memory_space=pltpu.SMEM)
    moba = pl.pallas_call(
        _moba_bias_kernel,
        grid=(N_HEADS_MOBA,),
        in_specs=[smem],
        out_specs=pl.BlockSpec((None, 2, MOBA_BLOCK, MOBA_BLOCK), lambda h: (h, 0, 0, 0)),
        out_shape=jax.ShapeDtypeStruct((N_HEADS_MOBA, 2, MOBA_BLOCK, MOBA_BLOCK), F32),
        compiler_params=pltpu.CompilerParams(dimension_semantics=("arbitrary",)),
        name="moba_bias",
    )(rel_bias)
    swa = pl.pallas_call(
        _swa_bias_kernel,
        grid=(N_HEADS_SWA,),
        in_specs=[smem],
        out_specs=pl.BlockSpec((None, 2 * SWA_WINDOW, SWA_WINDOW),
                               lambda h: (h // GQA_GROUP, 0, h % GQA_GROUP)),
        out_shape=jax.ShapeDtypeStruct((N_KV_SWA, 2 * SWA_WINDOW, GQA_GROUP * SWA_WINDOW), F32),
        compiler_params=pltpu.CompilerParams(dimension_semantics=("arbitrary",)),
        name="swa_bias",
    )(rel_bias)
    return moba, swa


def _inproj_kernel(x_ref, mod_ref, g_ref, w_ref, proj_ref, kmean_ref):
    t = pl.program_id(1)
    h = _rms(x_ref[...], g_ref[0:1, :]) * (1.0 + mod_ref[1:2, :]) + mod_ref[0:1, :]
    hb = h.astype(BF16)
    blocks_per_tile = ROW_TILE // MOBA_BLOCK
    for c0 in range(0, D_IN, W_A):
        c1 = min(c0 + W_A, D_IN)
        p = jnp.dot(hb, w_ref[:, c0:c1], preferred_element_type=F32)
        proj_ref[:, c0:c1] = p.astype(BF16)
        if c0 == W_A:
            for r in range(blocks_per_tile):
                kmean_ref[pl.ds(t * blocks_per_tile + r, 1), :] = jnp.mean(
                    p[r * MOBA_BLOCK:(r + 1) * MOBA_BLOCK], axis=0, keepdims=True)


def _inproj(x, mod, gains, w_in, l):
    bsz, seq, d = x.shape
    nb = seq // MOBA_BLOCK
    return pl.pallas_call(
        _inproj_kernel,
        grid=(bsz, seq // ROW_TILE),
        in_specs=[
            pl.BlockSpec((None, ROW_TILE, d), lambda b, t: (b, t, 0)),
            pl.BlockSpec((None, None, 6, d), lambda b, t: (l, b, 0, 0)),
            pl.BlockSpec((None, 4, d), lambda b, t: (l, 0, 0)),
            pl.BlockSpec((d, D_IN), lambda b, t: (0, 0)),
        ],
        out_specs=[
            pl.BlockSpec((None, ROW_TILE, D_IN), lambda b, t: (b, t, 0)),
            pl.BlockSpec((None, nb, W_A), lambda b, t: (b, 0, 0)),
        ],
        out_shape=[
            jax.ShapeDtypeStruct((bsz, seq, D_IN), BF16),
            jax.ShapeDtypeStruct((bsz, nb, W_A), F32),
        ],
        compiler_params=pltpu.CompilerParams(
            dimension_semantics=("arbitrary", "arbitrary"), vmem_limit_bytes=VMEM_LIMIT),
        name="inproj",
    )(x, mod, gains, w_in)


def _moba_kernel(cfar_ref, q_ref, k_ref, v_ref, km_ref, bias_ref, o_ref,
                 qt_scr, kh_scr, vt_scr, *, nb):
    hp = pl.program_id(1)
    seq = nb * MOBA_BLOCK
    lane = lax.broadcasted_iota(jnp.int32, (1, LANES), 1)
    first_head = lane < HEAD_DIM

    pair_w = 2 * MOBA_BLOCK
    j_io = lax.broadcasted_iota(jnp.int32, (nb, pair_w), 0)
    blk_in_pair = lax.broadcasted_iota(jnp.int32, (nb, pair_w), 1) // MOBA_BLOCK
    km = km_ref[...]
    gate_lhs = []
    for h in range(2):
        kmh = jnp.where(first_head, km, 0.0) if h == 0 else jnp.where(first_head, 0.0, km)
        p0 = kmh.astype(BF16)
        r1 = kmh - p0.astype(F32)
        p1 = r1.astype(BF16)
        p2 = (r1 - p1.astype(F32)).astype(BF16)
        gate_lhs.append(jnp.concatenate([p0, p1, p2, jnp.zeros_like(p0)], axis=0))
    n_ind = 2 * nb
    pad = jnp.zeros((HEAD_DIM - n_ind, pair_w), F32)
    tail = jnp.where(lax.broadcasted_iota(jnp.int32, (VT_ROWS - HEAD_DIM, MOBA_BLOCK), 0) == 0, 1.0, 0.0)

    def block_masks(ip, h, q_pair):
        qblk = 2 * ip + blk_in_pair
        valid = j_io < qblk
        g4 = lax.dot_general(gate_lhs[h], q_pair, (((1,), (1,)), ((), ())), preferred_element_type=F32)
        gate = g4[0:nb] + g4[nb:2 * nb] + g4[2 * nb:3 * nb]
        gm = jnp.where(valid, gate, NEG_INF)
        cnt = jnp.zeros((nb, pair_w), jnp.int32)
        for jp in range(nb):
            other = gm[jp:jp + 1, :]
            beats = (other > gm) | ((other == gm) & (jp < j_io))
            cnt = cnt + beats.astype(jnp.int32)
        sel = valid & (jnp.abs(gate) < jnp.inf) & (cnt < MOBA_TOPK)
        cfar2 = cfar_ref[2 * hp + h] * LOG2E
        keep_zero = (j_io == qblk) | (sel & (j_io == qblk - 1))
        am = jnp.where(keep_zero, 0.0, jnp.where(sel, cfar2, MASKED))
        am_hi = am.astype(BF16).astype(F32)
        am_lo = (am - am_hi).astype(BF16).astype(F32)
        return am_hi, am_lo

    def prepare(ip):
        q_pair = q_ref[ip * pair_w:(ip + 1) * pair_w, :]
        q_t = (q_pair.astype(F32) * (SCALE * LOG2E)).T
        hi0, lo0 = block_masks(ip, 0, q_pair)
        qt_scr[0, ip] = jnp.concatenate([q_t[0:HEAD_DIM], hi0, lo0, pad], axis=0).astype(BF16)
        hi1, lo1 = block_masks(ip, 1, q_pair)
        qt_scr[1, ip] = jnp.concatenate([hi1, lo1, pad, q_t[HEAD_DIM:]], axis=0).astype(BF16)
        for i in (2 * ip, 2 * ip + 1):
            rows = slice(i * MOBA_BLOCK, (i + 1) * MOBA_BLOCK)
            k_i = k_ref[rows, :].astype(F32)
            ind0 = jnp.where((lane == HEAD_DIM + i) | (lane == HEAD_DIM + nb + i), 1.0, 0.0)
            ind1 = jnp.where((lane == i) | (lane == nb + i), 1.0, 0.0)
            kh_scr[0, rows, :] = jnp.where(first_head, k_i, ind0).astype(BF16)
            kh_scr[1, rows, :] = jnp.where(first_head, ind1, k_i).astype(BF16)
            v_t = v_ref[rows, :].astype(F32).T
            vt_scr[0, :, rows] = jnp.concatenate([v_t[0:HEAD_DIM], tail], axis=0).astype(BF16)
            vt_scr[1, :, rows] = jnp.concatenate([v_t[HEAD_DIM:], tail], axis=0).astype(BF16)

    blk = MOBA_BLOCK

    def pair_scores(ip, h):
        i = 2 * ip
        s = jnp.dot(kh_scr[h, 0:(i + 1) * blk, :], qt_scr[h, ip], preferred_element_type=F32)
        tiles = []
        for j in range(i + 1):
            s_j = s[j * blk:(j + 1) * blk]
            if j == i:
                s_j = s_j + jnp.concatenate([bias_ref[h, 0], bias_ref[h, 1]], axis=1)
            elif j == i - 1:
                s_j = jnp.concatenate([s_j[:, 0:blk] + bias_ref[h, 1], s_j[:, blk:]], axis=1)
            tiles.append(s_j)
        last = jnp.dot(kh_scr[h, (i + 1) * blk:(i + 2) * blk, :], qt_scr[h, ip, :, blk:2 * blk],
                       preferred_element_type=F32) + bias_ref[h, 0]
        m = functools.reduce(jnp.maximum, [jnp.max(t, axis=0, keepdims=True) for t in tiles])
        m = jnp.concatenate(
            [m[:, 0:blk], jnp.maximum(m[:, blk:], jnp.max(last, axis=0, keepdims=True))], axis=1)
        return tiles, last, m

    def pair_values(ip, h, tiles, last, m):
        i = 2 * ip
        p = jnp.concatenate([jnp.exp2(t - m).astype(BF16) for t in tiles], axis=0)
        acc = jnp.dot(vt_scr[h, :, 0:(i + 1) * blk], p, preferred_element_type=F32)
        p_last = jnp.exp2(last - m[:, blk:]).astype(BF16)
        acc_last = jnp.dot(vt_scr[h, :, (i + 1) * blk:(i + 2) * blk], p_last, preferred_element_type=F32)
        return acc[:, 0:blk], acc[:, blk:] + acc_last

    def finish(i, acc0, acc1):
        out_t = jnp.concatenate(
            [acc0[0:HEAD_DIM] * (1.0 / acc0[HEAD_DIM:HEAD_DIM + 1]),
             acc1[0:HEAD_DIM] * (1.0 / acc1[HEAD_DIM:HEAD_DIM + 1])], axis=0)
        o_ref[i * blk:(i + 1) * blk, :] = out_t.T.astype(BF16)

    units = [(ip, h) for ip in range(nb // 2) for h in range(2)]
    accs = {}
    cur = None
    prepare(0)
    for unit in units + [None]:
        nxt = (unit, pair_scores(*unit)) if unit is not None else None
        if unit is not None and unit[1] == 0 and unit[0] + 1 < nb // 2:
            prepare(unit[0] + 1)
        if cur is not None:
            (ip, h), st = cur
            accs[h] = pair_values(ip, h, *st)
            if h == 1:
                finish(2 * ip, accs[0][0], accs[1][0])
                finish(2 * ip + 1, accs[0][1], accs[1][1])
        cur = nxt


def _moba(proj, kmean, bias_moba, cfar):
    bsz, seq, _ = proj.shape
    nb = seq // MOBA_BLOCK
    n_pairs = W_A // LANES
    kern = functools.partial(_moba_kernel, nb=nb)
    return pl.pallas_call(
        kern,
        grid=(bsz, n_pairs),
        in_specs=[
            pl.BlockSpec(memory_space=pltpu.SMEM),
            pl.BlockSpec((None, seq, LANES), lambda b, hp: (b, 0, hp)),
            pl.BlockSpec((None, seq, LANES), lambda b, hp: (b, 0, n_pairs + hp)),
            pl.BlockSpec((None, seq, LANES), lambda b, hp: (b, 0, 2 * n_pairs + hp)),
            pl.BlockSpec((None, nb, LANES), lambda b, hp: (b, 0, hp)),
            pl.BlockSpec((2, 2, MOBA_BLOCK, MOBA_BLOCK), lambda b, hp: (hp, 0, 0, 0)),
        ],
        out_specs=pl.BlockSpec((None, seq, LANES), lambda b, hp: (b, 0, hp)),
        out_shape=jax.ShapeDtypeStruct((bsz, seq, W_A), BF16),
        scratch_shapes=[
            pltpu.VMEM((2, nb // 2, LANES, 2 * MOBA_BLOCK), BF16),
            pltpu.VMEM((2, seq, LANES), BF16),
            pltpu.VMEM((2, VT_ROWS, seq), BF16),
        ],
        compiler_params=pltpu.CompilerParams(
            dimension_semantics=("arbitrary", "arbitrary"), vmem_limit_bytes=VMEM_LIMIT),
        name="moba",
    )(cfar, proj, proj, proj, kmean, bias_moba)


def _swa_kernel(sink_ref, q_ref, k_ref, v_ref, bias_ref, o_ref, qt_scr, kh_scr, vt_scr, *, nbq, layer):
    w = SWA_WINDOW
    gw = GQA_GROUP * w
    lane = lax.broadcasted_iota(jnp.int32, (1, LANES), 1)
    first_half = lane < HEAD_DIM
    zeros_q = jnp.zeros((HEAD_DIM, gw), F32)
    tail = jnp.where(lax.broadcasted_iota(jnp.int32, (VT_ROWS - HEAD_DIM, w), 0) == 0, 1.0, 0.0)

    for n in range(nbq):
        rows = slice(n * w, (n + 1) * w)
        k_n = k_ref[rows, :].astype(F32)
        kh_scr[0, rows, :] = jnp.where(first_half, k_n, 0.0).astype(BF16)
        kh_scr[1, rows, :] = jnp.where(first_half, 0.0, k_n).astype(BF16)
        v_t = v_ref[rows, :].astype(F32).T
        vt_scr[0, :, rows] = jnp.concatenate([v_t[0:HEAD_DIM], tail], axis=0).astype(BF16)
        vt_scr[1, :, rows] = jnp.concatenate([v_t[HEAD_DIM:], tail], axis=0).astype(BF16)
        q_t = (q_ref[rows, :].astype(F32) * (SCALE * LOG2E)).T
        for g in range(N_KV_SWA):
            heads = [q_t[(g * GQA_GROUP + hh) * HEAD_DIM:(g * GQA_GROUP + hh + 1) * HEAD_DIM]
                     for hh in range(GQA_GROUP)]
            q_g = jnp.concatenate(heads, axis=1)
            parts = [q_g, zeros_q] if g == 0 else [zeros_q, q_g]
            qt_scr[g, n] = jnp.concatenate(parts, axis=0).astype(BF16)

    group_lane = lax.broadcasted_iota(jnp.int32, (1, gw), 1) // w
    sink_rows = []
    for g in range(N_KV_SWA):
        sr = jnp.zeros((1, gw), F32)
        for hh in range(GQA_GROUP):
            sr = jnp.where(group_lane == hh, sink_ref[layer, g * GQA_GROUP + hh] * LOG2E, sr)
        sink_rows.append(sr)

    def scores(n, g):
        k0 = max(n - 1, 0) * w
        s = jnp.dot(kh_scr[g, k0:(n + 1) * w, :], qt_scr[g, n], preferred_element_type=F32)
        s = s + (bias_ref[g] if n > 0 else bias_ref[g, w:2 * w, :])
        m = jnp.maximum(jnp.max(s, axis=0, keepdims=True), sink_rows[g])
        return s, m

    def finish(n, g, s, m):
        k0 = max(n - 1, 0) * w
        p = jnp.exp2(s - m).astype(BF16)
        acc = jnp.dot(vt_scr[g, :, k0:(n + 1) * w], p, preferred_element_type=F32)
        l = acc[HEAD_DIM:HEAD_DIM + 1] + jnp.exp2(sink_rows[g] - m)
        o = acc[0:HEAD_DIM] * (1.0 / l)
        for pp in range(GQA_GROUP // 2):
            pair_t = jnp.concatenate([o[:, (2 * pp) * w:(2 * pp + 1) * w],
                                      o[:, (2 * pp + 1) * w:(2 * pp + 2) * w]], axis=0)
            col = (g * (GQA_GROUP // 2) + pp) * LANES
            o_ref[n * w:(n + 1) * w, col:col + LANES] = pair_t.T.astype(BF16)

    units = [(n, g) for n in range(nbq) for g in range(N_KV_SWA)]
    cur = None
    for unit in units + [None]:
        nxt = (unit, scores(*unit)) if unit is not None else None
        if cur is not None:
            finish(*cur[0], *cur[1])
        cur = nxt


def _swa(proj, bias_swa, sinks, l):
    bsz, seq, _ = proj.shape
    nbq = seq // SWA_WINDOW
    kern = functools.partial(_swa_kernel, nbq=nbq, layer=l)
    q_col = (3 * W_A) // W_QB
    k_col = (3 * W_A + W_QB) // W_KVB
    return pl.pallas_call(
        kern,
        grid=(bsz,),
        in_specs=[
            pl.BlockSpec(memory_space=pltpu.SMEM),
            pl.BlockSpec((None, seq, W_QB), lambda b: (b, 0, q_col)),
            pl.BlockSpec((None, seq, W_KVB), lambda b: (b, 0, k_col)),
            pl.BlockSpec((None, seq, W_KVB), lambda b: (b, 0, k_col + 1)),
            pl.BlockSpec((N_KV_SWA, 2 * SWA_WINDOW, GQA_GROUP * SWA_WINDOW), lambda b: (0, 0, 0)),
        ],
        out_specs=pl.BlockSpec((None, seq, W_QB), lambda b: (b, 0, 0)),
        out_shape=jax.ShapeDtypeStruct((bsz, seq, W_QB), BF16),
        scratch_shapes=[
            pltpu.VMEM((N_KV_SWA, nbq, LANES, GQA_GROUP * SWA_WINDOW), BF16),
            pltpu.VMEM((N_KV_SWA, seq, LANES), BF16),
            pltpu.VMEM((N_KV_SWA, VT_ROWS, seq), BF16),
        ],
        compiler_params=pltpu.CompilerParams(
            dimension_semantics=("arbitrary",), vmem_limit_bytes=VMEM_LIMIT),
        name="swa",
    )(sinks, proj, proj, proj, bias_swa)


def _mix_mlp_kernel(*refs, convert_next):
    a_ref, b_ref, x_ref, mod_ref, g_ref, wo_ref, w1_ref, b1_ref, w2_ref, b2_ref = refs[:10]
    n_conv = 4 if convert_next else 0
    src_refs = refs[10:10 + n_conv]
    o_ref = refs[10 + n_conv]
    dst_refs = refs[11 + n_conv:]

    y = (jnp.dot(a_ref[...], wo_ref[0:W_A, :], preferred_element_type=F32)
         + jnp.dot(b_ref[...], wo_ref[W_A:, :], preferred_element_type=F32))
    x = x_ref[...] + mod_ref[2:3, :] * _rms(y, g_ref[1:2, :])
    h = _rms(x, g_ref[2:3, :]) * (1.0 + mod_ref[4:5, :]) + mod_ref[3:4, :]
    hb = h.astype(BF16)
    acc = jnp.zeros(x.shape, F32)
    for c0 in range(0, D_FF, FF_CHUNK):
        u = jnp.dot(hb, w1_ref[:, c0:c0 + FF_CHUNK], preferred_element_type=F32) + b1_ref[:, c0:c0 + FF_CHUNK]
        u = jnp.square(jnp.maximum(u, 0.0))
        acc = acc + jnp.dot(u.astype(BF16), w2_ref[c0:c0 + FF_CHUNK, :], preferred_element_type=F32)
    y2 = acc + b2_ref[...]
    o_ref[...] = x + mod_ref[5:6, :] * _rms(y2, g_ref[3:4, :])
    for src_ref, dst_ref in zip(src_refs, dst_refs):
        dst_ref[...] = src_ref[...].astype(BF16)


def _mix_mlp(attn_a, attn_b, x, mod, gains, w_out, w1, b1, w2, b2, l, next_f32=None):
    bsz, seq, d = x.shape
    tps = seq // ROW_TILE
    n_steps = bsz * tps
    const = lambda shape: pl.BlockSpec(shape, lambda b, t: (0,) * len(shape), pipeline_mode=pl.Buffered(1))
    layer_const = lambda shape: pl.BlockSpec((None,) + shape, lambda b, t: (l,) + (0,) * len(shape))
    in_specs = [
        pl.BlockSpec((None, ROW_TILE, W_A), lambda b, t: (b, t, 0)),
        pl.BlockSpec((None, ROW_TILE, W_QB), lambda b, t: (b, t, 0)),
        pl.BlockSpec((None, ROW_TILE, d), lambda b, t: (b, t, 0)),
        pl.BlockSpec((None, None, 6, d), lambda b, t: (l, b, 0, 0)),
        layer_const((4, d)),
        const((d, d)),
        const((d, D_FF)),
        layer_const((1, D_FF)),
        const((D_FF, d)),
        layer_const((1, d)),
    ]
    args = [attn_a, attn_b, x, mod, gains, w_out, w1, b1, w2, b2]
    out_specs = [pl.BlockSpec((None, ROW_TILE, d), lambda b, t: (b, t, 0))]
    out_shape = [jax.ShapeDtypeStruct((bsz, seq, d), F32)]
    if next_f32 is not None:
        for w in next_f32:
            rows, cols = w.shape[1] // n_steps, w.shape[2]
            assert w.shape[1] % n_steps == 0 and rows % 16 == 0
            in_specs.append(pl.BlockSpec((None, rows, cols), lambda b, t: (l + 1, b * tps + t, 0)))
            args.append(w)
            out_specs.append(pl.BlockSpec((rows, cols), lambda b, t: (b * tps + t, 0)))
            out_shape.append(jax.ShapeDtypeStruct(w.shape[1:], BF16))
    outs = pl.pallas_call(
        functools.partial(_mix_mlp_kernel, convert_next=next_f32 is not None),
        grid=(bsz, tps),
        in_specs=in_specs,
        out_specs=out_specs,
        out_shape=out_shape,
        compiler_params=pltpu.CompilerParams(
            dimension_semantics=("arbitrary", "arbitrary"), vmem_limit_bytes=VMEM_LIMIT),
        name="mix_mlp",
    )(*args)
    return outs if next_f32 is not None else outs[0]


def kernel(x, c, w_in, w_out, sinks, rel_bias, w_ada, b_ada, norm_gains, w1, b1, w2, b2):
    bsz, seq, d = x.shape
    depth = w_in.shape[0]
    assert d == D_MODEL and seq % ROW_TILE == 0 and ROW_TILE % MOBA_BLOCK == 0
    assert seq // MOBA_BLOCK == 8, "the block-gate matmul packs 8 block means per bf16 tile row group"

    mod = _adaln(c, w_ada, b_ada).reshape(depth, bsz, 6, d)
    bias_moba, bias_swa = _bias_tiles(rel_bias)
    cfar = rel_bias[N_BUCKETS - 1, :N_HEADS_MOBA]

    w_in_b, w_out_b, w1_b, w2_b = (w[0].astype(BF16) for w in (w_in, w_out, w1, w2))
    b1_r = b1.reshape(depth, 1, D_FF)
    b2_r = b2.reshape(depth, 1, d)

    for l in range(depth):
        proj, kmean = _inproj(x, mod, norm_gains, w_in_b, l)
        attn_a = _moba(proj, kmean, bias_moba, cfar)
        attn_b = _swa(proj, bias_swa, sinks, l)
        if l + 1 < depth:
            x, w_in_b, w_out_b, w1_b, w2_b = _mix_mlp(
                attn_a, attn_b, x, mod, norm_gains, w_out_b, w1_b, b1_r, w2_b, b2_r, l,
                next_f32=(w_in, w_out, w1, w2))
        else:
            x = _mix_mlp(attn_a, attn_b, x, mod, norm_gains, w_out_b, w1_b, b1_r, w2_b, b2_r, l)
    return x
```

```python
import functools
import math

import jax
import jax.numpy as jnp
from jax import lax
from jax.experimental import pallas as pl
from jax.experimental.pallas import tpu as pltpu

D_MODEL = 1024
HEAD_DIM = 64
N_HEADS_MOBA = 8
N_HEADS_SWA = 8
N_KV_SWA = 2
GQA_GROUP = N_HEADS_SWA // N_KV_SWA
MOBA_BLOCK = 256
MOBA_TOPK = 3
SWA_WINDOW = 128
N_BUCKETS = 32
MAX_DISTANCE = 128
D_FF = 4 * D_MODEL
EPS = 1e-6
W_A = N_HEADS_MOBA * HEAD_DIM
W_QB = N_HEADS_SWA * HEAD_DIM
W_KVB = N_KV_SWA * HEAD_DIM
D_IN = 3 * W_A + W_QB + 2 * W_KVB
SCALE = HEAD_DIM ** -0.5

LANES = 128
VMEM_LIMIT = 56 * 1024 * 1024
ROW_TILE = 512
FF_CHUNK = 1024
ADA_COL_TILE = 2048

F32 = jnp.float32
BF16 = jnp.bfloat16
NEG_INF = float("-inf")
LOG2E = 1.4426950408889634
MASKED = -1e30
VT_ROWS = 80


def _rms(x, row):
    return x * lax.rsqrt(jnp.mean(x * x, axis=-1, keepdims=True) + EPS) * row


def _t5_bucket(dist):
    n = jnp.maximum(dist, 0)
    max_exact = N_BUCKETS // 2
    nf = jnp.maximum(n, 1).astype(F32)
    large = max_exact + (jnp.log(nf / max_exact) / math.log(MAX_DISTANCE / max_exact)
                         * (N_BUCKETS - max_exact)).astype(jnp.int32)
    large = jnp.minimum(large, N_BUCKETS - 1)
    return jnp.where(n < max_exact, n, large)


def _bias_by_distance(width, rb_ref, head):
    dist = lax.broadcasted_iota(jnp.int32, (8, width), 1)
    bucket = _t5_bucket(dist)
    val = jnp.zeros(dist.shape, F32)
    for b in range(N_BUCKETS):
        val = jnp.where(bucket == b, rb_ref[b, head], val)
    return val[0:1, :] * LOG2E


def _toeplitz(table_row, n_keys, offset):
    width = table_row.shape[1]
    rows = jnp.broadcast_to(table_row, (n_keys, width))
    return pltpu.roll(rows, (width - offset) % width, 1, stride=1, stride_axis=0)


def _adaln_kernel(c_ref, w_ref, b_ref, o_ref):
    c = c_ref[...]
    c_act = c * (1.0 / (1.0 + jnp.exp(-c)))
    o_ref[...] = jnp.dot(c_act.astype(BF16), w_ref[...].astype(BF16),
                         preferred_element_type=F32) + b_ref[...]


def _adaln(c, w_ada, b_ada):
    depth, d, n = w_ada.shape
    bsz = c.shape[0]
    return pl.pallas_call(
        _adaln_kernel,
        grid=(depth, n // ADA_COL_TILE),
        in_specs=[
            pl.BlockSpec((bsz, d), lambda l, j: (0, 0)),
            pl.BlockSpec((None, d, ADA_COL_TILE), lambda l, j: (l, 0, j)),
            pl.BlockSpec((None, 1, ADA_COL_TILE), lambda l, j: (l, 0, j)),
        ],
        out_specs=pl.BlockSpec((None, bsz, ADA_COL_TILE), lambda l, j: (l, 0, j)),
        out_shape=jax.ShapeDtypeStruct((depth, bsz, n), F32),
        compiler_params=pltpu.CompilerParams(
            dimension_semantics=("arbitrary", "arbitrary"), vmem_limit_bytes=VMEM_LIMIT),
        name="adaln",
    )(c, w_ada, b_ada.reshape(depth, 1, n))


def _moba_bias_kernel(rb_ref, o_ref):
    h = pl.program_id(0)
    table = _bias_by_distance(2 * MOBA_BLOCK, rb_ref, h)
    kk = lax.broadcasted_iota(jnp.int32, (MOBA_BLOCK, MOBA_BLOCK), 0)
    qq = lax.broadcasted_iota(jnp.int32, (MOBA_BLOCK, MOBA_BLOCK), 1)
    o_ref[0] = jnp.where(qq >= kk, _toeplitz(table, MOBA_BLOCK, 0)[:, 0:MOBA_BLOCK], NEG_INF)
    o_ref[1] = _toeplitz(table, MOBA_BLOCK, MOBA_BLOCK)[:, 0:MOBA_BLOCK]


def _swa_bias_kernel(rb_ref, o_ref):
    h = pl.program_id(0) + N_HEADS_MOBA
    table = _bias_by_distance(2 * SWA_WINDOW, rb_ref, h)
    jj = lax.broadcasted_iota(jnp.int32, (2 * SWA_WINDOW, SWA_WINDOW), 0)
    qq = lax.broadcasted_iota(jnp.int32, (2 * SWA_WINDOW, SWA_WINDOW), 1)
    dist = qq + SWA_WINDOW - jj
    allowed = (dist >= 0) & (dist < SWA_WINDOW)
    o_ref[...] = jnp.where(allowed, _toeplitz(table, 2 * SWA_WINDOW, SWA_WINDOW)[:, 0:SWA_WINDOW], NEG_INF)


def _bias_tiles(rel_bias):
    smem = pl.BlockSpec(memory_space=pltpu.SMEM)
    moba = pl.pallas_call(
        _moba_bias_kernel,
        grid=(N_HEADS_MOBA,),
        in_specs=[smem],
        out_specs=pl.BlockSpec((None, 2, MOBA_BLOCK, MOBA_BLOCK), lambda h: (h, 0, 0, 0)),
        out_shape=jax.ShapeDtypeStruct((N_HEADS_MOBA, 2, MOBA_BLOCK, MOBA_BLOCK), F32),
        compiler_params=pltpu.CompilerParams(dimension_semantics=("arbitrary",)),
        name="moba_bias",
    )(rel_bias)
    swa = pl.pallas_call(
        _swa_bias_kernel,
        grid=(N_HEADS_SWA,),
        in_specs=[smem],
        out_specs=pl.BlockSpec((None, 2 * SWA_WINDOW, SWA_WINDOW),
                               lambda h: (h // GQA_GROUP, 0, h % GQA_GROUP)),
        out_shape=jax.ShapeDtypeStruct((N_KV_SWA, 2 * SWA_WINDOW, GQA_GROUP * SWA_WINDOW), F32),
        compiler_params=pltpu.CompilerParams(dimension_semantics=("arbitrary",)),
        name="swa_bias",
    )(rel_bias)
    return moba, swa


def _cast_chunk_specs(weights, layer, bsz, tps):
    n_steps = bsz * tps
    in_specs, out_specs, out_shape = [], [], []
    for w in weights:
        rows, cols = w.shape[1] // n_steps, w.shape[2]
        assert w.shape[1] % n_steps == 0 and rows % 16 == 0
        in_specs.append(pl.BlockSpec((None, rows, cols), lambda b, t: (layer, b * tps + t, 0)))
        out_specs.append(pl.BlockSpec((rows, cols), lambda b, t: (b * tps + t, 0)))
        out_shape.append(jax.ShapeDtypeStruct(w.shape[1:], BF16))
    return in_specs, out_specs, out_shape


def _inproj_kernel(*refs, n_cast):
    x_ref, mod_ref, g_ref, w_ref = refs[:4]
    src_refs = refs[4:4 + n_cast]
    proj_ref, kmean_ref = refs[4 + n_cast:6 + n_cast]
    dst_refs = refs[6 + n_cast:]
    t = pl.program_id(1)
    h = _rms(x_ref[...], g_ref[0:1, :] * (1.0 + mod_ref[1:2, :])) + mod_ref[0:1, :]
    hb = h.astype(BF16)
    blocks_per_tile = ROW_TILE // MOBA_BLOCK
    for c0 in range(0, D_IN, W_A):
        c1 = min(c0 + W_A, D_IN)
        p = jnp.dot(hb, w_ref[:, c0:c1], preferred_element_type=F32)
        proj_ref[:, c0:c1] = p.astype(BF16)
        if c0 == W_A:
            for r in range(blocks_per_tile):
                kmean_ref[pl.ds(t * blocks_per_tile + r, 1), :] = jnp.mean(
                    p[r * MOBA_BLOCK:(r + 1) * MOBA_BLOCK], axis=0, keepdims=True)
    for src_ref, dst_ref in zip(src_refs, dst_refs):
        dst_ref[...] = src_ref[...].astype(BF16)


def _inproj(x, mod, gains, w_in, l, cast_f32=()):
    bsz, seq, d = x.shape
    nb = seq // MOBA_BLOCK
    tps = seq // ROW_TILE
    cast_in, cast_out, cast_shape = _cast_chunk_specs(cast_f32, l, bsz, tps)
    outs = pl.pallas_call(
        functools.partial(_inproj_kernel, n_cast=len(cast_f32)),
        grid=(bsz, tps),
        in_specs=[
            pl.BlockSpec((None, ROW_TILE, d), lambda b, t: (b, t, 0)),
            pl.BlockSpec((None, None, 6, d), lambda b, t: (l, b, 0, 0)),
            pl.BlockSpec((None, 4, d), lambda b, t: (l, 0, 0)),
            pl.BlockSpec((d, D_IN), lambda b, t: (0, 0)),
        ] + cast_in,
        out_specs=[
            pl.BlockSpec((None, ROW_TILE, D_IN), lambda b, t: (b, t, 0)),
            pl.BlockSpec((None, nb, W_A), lambda b, t: (b, 0, 0)),
        ] + cast_out,
        out_shape=[
            jax.ShapeDtypeStruct((bsz, seq, D_IN), BF16),
            jax.ShapeDtypeStruct((bsz, nb, W_A), F32),
        ] + cast_shape,
        compiler_params=pltpu.CompilerParams(
            dimension_semantics=("arbitrary", "arbitrary"), vmem_limit_bytes=VMEM_LIMIT),
        name="inproj",
    )(x, mod, gains, w_in, *cast_f32)
    return outs[0], outs[1], tuple(outs[2:])


def _moba_kernel(cfar_ref, q_ref, k_ref, v_ref, km_ref, bias_ref, o_ref,
                 qt_scr, kh_scr, vt_scr, *, nb):
    hp = pl.program_id(1)
    seq = nb * MOBA_BLOCK
    lane = lax.broadcasted_iota(jnp.int32, (1, LANES), 1)
    first_head = lane < HEAD_DIM

    pair_w = 2 * MOBA_BLOCK
    j_io = lax.broadcasted_iota(jnp.int32, (nb, pair_w), 0)
    blk_in_pair = lax.broadcasted_iota(jnp.int32, (nb, pair_w), 1) // MOBA_BLOCK
    km = km_ref[...]
    gate_lhs = []
    for h in range(2):
        kmh = jnp.where(first_head, km, 0.0) if h == 0 else jnp.where(first_head, 0.0, km)
        p0 = kmh.astype(BF16)
        r1 = kmh - p0.astype(F32)
        p1 = r1.astype(BF16)
        p2 = (r1 - p1.astype(F32)).astype(BF16)
        gate_lhs.append(jnp.concatenate([p0, p1, p2, jnp.zeros_like(p0)], axis=0))
    n_ind = 2 * nb
    pad = jnp.zeros((HEAD_DIM - n_ind, pair_w), F32)
    tail = jnp.where(lax.broadcasted_iota(jnp.int32, (VT_ROWS - HEAD_DIM, MOBA_BLOCK), 0) == 0, 1.0, 0.0)

    def block_masks(ip, h, q_pair):
        qblk = 2 * ip + blk_in_pair
        valid = j_io < qblk
        g4 = lax.dot_general(gate_lhs[h], q_pair, (((1,), (1,)), ((), ())), preferred_element_type=F32)
        gate = g4[0:nb] + g4[nb:2 * nb] + g4[2 * nb:3 * nb]
        gm = jnp.where(valid, gate, NEG_INF)
        cnt = jnp.zeros((nb, pair_w), jnp.int32)
        for jp in range(nb):
            other = gm[jp:jp + 1, :]
            beats = (other > gm) | ((other == gm) & (jp < j_io))
            cnt = cnt + beats.astype(jnp.int32)
        sel = valid & (jnp.abs(gate) < jnp.inf) & (cnt < MOBA_TOPK)
        cfar2 = cfar_ref[2 * hp + h] * LOG2E
        keep_zero = (j_io == qblk) | (sel & (j_io == qblk - 1))
        am = jnp.where(keep_zero, 0.0, jnp.where(sel, cfar2, MASKED))
        am_hi = am.astype(BF16).astype(F32)
        am_lo = (am - am_hi).astype(BF16).astype(F32)
        return am_hi, am_lo

    def prepare(ip):
        q_pair = q_ref[ip * pair_w:(ip + 1) * pair_w, :]
        q_t = (q_pair.astype(F32) * (SCALE * LOG2E)).T
        hi0, lo0 = block_masks(ip, 0, q_pair)
        qt_scr[0, ip] = jnp.concatenate([q_t[0:HEAD_DIM], hi0, lo0, pad], axis=0).astype(BF16)
        hi1, lo1 = block_masks(ip, 1, q_pair)
        qt_scr[1, ip] = jnp.concatenate([hi1, lo1, pad, q_t[HEAD_DIM:]], axis=0).astype(BF16)
        for i in (2 * ip, 2 * ip + 1):
            rows = slice(i * MOBA_BLOCK, (i + 1) * MOBA_BLOCK)
            k_i = k_ref[rows, :].astype(F32)
            ind0 = jnp.where((lane == HEAD_DIM + i) | (lane == HEAD_DIM + nb + i), 1.0, 0.0)
            ind1 = jnp.where((lane == i) | (lane == nb + i), 1.0, 0.0)
            kh_scr[0, rows, :] = jnp.where(first_head, k_i, ind0).astype(BF16)
            kh_scr[1, rows, :] = jnp.where(first_head, ind1, k_i).astype(BF16)
            v_t = v_ref[rows, :].astype(F32).T
            vt_scr[0, :, rows] = jnp.concatenate([v_t[0:HEAD_DIM], tail], axis=0).astype(BF16)
            vt_scr[1, :, rows] = jnp.concatenate([v_t[HEAD_DIM:], tail], axis=0).astype(BF16)

    blk = MOBA_BLOCK

    def pair_scores(ip, h):
        i = 2 * ip
        s = jnp.dot(kh_scr[h, 0:(i + 1) * blk, :], qt_scr[h, ip], preferred_element_type=F32)
        tiles = []
        for j in range(i + 1):
            s_j = s[j * blk:(j + 1) * blk]
            if j == i:
                s_j = s_j + jnp.concatenate([bias_ref[h, 0], bias_ref[h, 1]], axis=1)
            elif j == i - 1:
                s_j = jnp.concatenate([s_j[:, 0:blk] + bias_ref[h, 1], s_j[:, blk:]], axis=1)
            tiles.append(s_j)
        last = jnp.dot(kh_scr[h, (i + 1) * blk:(i + 2) * blk, :], qt_scr[h, ip, :, blk:2 * blk],
                       preferred_element_type=F32) + bias_ref[h, 0]
        m = functools.reduce(jnp.maximum, [jnp.max(t, axis=0, keepdims=True) for t in tiles])
        m = jnp.concatenate(
            [m[:, 0:blk], jnp.maximum(m[:, blk:], jnp.max(last, axis=0, keepdims=True))], axis=1)
        return tiles, last, m

    def pair_values(ip, h, tiles, last, m):
        i = 2 * ip
        p = jnp.concatenate([jnp.exp2(t - m).astype(BF16) for t in tiles], axis=0)
        acc = jnp.dot(vt_scr[h, :, 0:(i + 1) * blk], p, preferred_element_type=F32)
        p_last = jnp.exp2(last - m[:, blk:]).astype(BF16)
        acc_last = jnp.dot(vt_scr[h, :, (i + 1) * blk:(i + 2) * blk], p_last, preferred_element_type=F32)
        return acc[:, 0:blk], acc[:, blk:] + acc_last

    def finish(i, acc0, acc1):
        out_t = jnp.concatenate(
            [acc0[0:HEAD_DIM] * (1.0 / acc0[HEAD_DIM:HEAD_DIM + 1]),
             acc1[0:HEAD_DIM] * (1.0 / acc1[HEAD_DIM:HEAD_DIM + 1])], axis=0)
        o_ref[i * blk:(i + 1) * blk, :] = out_t.T.astype(BF16)

    units = [(ip, h) for ip in range(nb // 2) for h in range(2)]
    accs = {}
    cur = None
    prepare(0)
    for unit in units + [None]:
        nxt = (unit, pair_scores(*unit)) if unit is not None else None
        if unit is not None and unit[1] == 0 and unit[0] + 1 < nb // 2:
            prepare(unit[0] + 1)
        if cur is not None:
            (ip, h), st = cur
            accs[h] = pair_values(ip, h, *st)
            if h == 1:
                finish(2 * ip, accs[0][0], accs[1][0])
                finish(2 * ip + 1, accs[0][1], accs[1][1])
        cur = nxt


def _moba(proj, kmean, bias_moba, cfar):
    bsz, seq, _ = proj.shape
    nb = seq // MOBA_BLOCK
    n_pairs = W_A // LANES
    kern = functools.partial(_moba_kernel, nb=nb)
    return pl.pallas_call(
        kern,
        grid=(bsz, n_pairs),
        in_specs=[
            pl.BlockSpec(memory_space=pltpu.SMEM),
            pl.BlockSpec((None, seq, LANES), lambda b, hp: (b, 0, hp)),
            pl.BlockSpec((None, seq, LANES), lambda b, hp: (b, 0, n_pairs + hp)),
            pl.BlockSpec((None, seq, LANES), lambda b, hp: (b, 0, 2 * n_pairs + hp)),
            pl.BlockSpec((None, nb, LANES), lambda b, hp: (b, 0, hp)),
            pl.BlockSpec((2, 2, MOBA_BLOCK, MOBA_BLOCK), lambda b, hp: (hp, 0, 0, 0)),
        ],
        out_specs=pl.BlockSpec((None, seq, LANES), lambda b, hp: (b, 0, hp)),
        out_shape=jax.ShapeDtypeStruct((bsz, seq, W_A), BF16),
        scratch_shapes=[
            pltpu.VMEM((2, nb // 2, LANES, 2 * MOBA_BLOCK), BF16),
            pltpu.VMEM((2, seq, LANES), BF16),
            pltpu.VMEM((2, VT_ROWS, seq), BF16),
        ],
        compiler_params=pltpu.CompilerParams(
            dimension_semantics=("arbitrary", "arbitrary"), vmem_limit_bytes=VMEM_LIMIT),
        name="moba",
    )(cfar, proj, proj, proj, kmean, bias_moba)


def _swa_kernel(sink_ref, q_ref, k_ref, v_ref, bias_ref, o_ref, qt_scr, kh_scr, vt_scr, *, nbq, layer):
    w = SWA_WINDOW
    gw = GQA_GROUP * w
    lane = lax.broadcasted_iota(jnp.int32, (1, LANES), 1)
    first_half = lane < HEAD_DIM
    zeros_q = jnp.zeros((HEAD_DIM, gw), F32)
    tail = jnp.where(lax.broadcasted_iota(jnp.int32, (VT_ROWS - HEAD_DIM, w), 0) == 0, 1.0, 0.0)

    for n in range(nbq):
        rows = slice(n * w, (n + 1) * w)
        k_n = k_ref[rows, :].astype(F32)
        kh_scr[0, rows, :] = jnp.where(first_half, k_n, 0.0).astype(BF16)
        kh_scr[1, rows, :] = jnp.where(first_half, 0.0, k_n).astype(BF16)
        v_t = v_ref[rows, :].astype(F32).T
        vt_scr[0, :, rows] = jnp.concatenate([v_t[0:HEAD_DIM], tail], axis=0).astype(BF16)
        vt_scr[1, :, rows] = jnp.concatenate([v_t[HEAD_DIM:], tail], axis=0).astype(BF16)
        q_t = (q_ref[rows, :].astype(F32) * (SCALE * LOG2E)).T
        for g in range(N_KV_SWA):
            heads = [q_t[(g * GQA_GROUP + hh) * HEAD_DIM:(g * GQA_GROUP + hh + 1) * HEAD_DIM]
                     for hh in range(GQA_GROUP)]
            q_g = jnp.concatenate(heads, axis=1)
            parts = [q_g, zeros_q] if g == 0 else [zeros_q, q_g]
            qt_scr[g, n] = jnp.concatenate(parts, axis=0).astype(BF16)

    group_lane = lax.broadcasted_iota(jnp.int32, (1, gw), 1) // w
    sink_rows = []
    for g in range(N_KV_SWA):
        sr = jnp.zeros((1, gw), F32)
        for hh in range(GQA_GROUP):
            sr = jnp.where(group_lane == hh, sink_ref[layer, g * GQA_GROUP + hh] * LOG2E, sr)
        sink_rows.append(sr)

    def scores(n, g):
        k0 = max(n - 1, 0) * w
        s = jnp.dot(kh_scr[g, k0:(n + 1) * w, :], qt_scr[g, n], preferred_element_type=F32)
        s = s + (bias_ref[g] if n > 0 else bias_ref[g, w:2 * w, :])
        m = jnp.maximum(jnp.max(s, axis=0, keepdims=True), sink_rows[g])
        return s, m

    def finish(n, g, s, m):
        k0 = max(n - 1, 0) * w
        p = jnp.exp2(s - m).astype(BF16)
        acc = jnp.dot(vt_scr[g, :, k0:(n + 1) * w], p, preferred_element_type=F32)
        l = acc[HEAD_DIM:HEAD_DIM + 1] + jnp.exp2(sink_rows[g] - m)
        o = acc[0:HEAD_DIM] * (1.0 / l)
        for pp in range(GQA_GROUP // 2):
            pair_t = jnp.concatenate([o[:, (2 * pp) * w:(2 * pp + 1) * w],
                                      o[:, (2 * pp + 1) * w:(2 * pp + 2) * w]], axis=0)
            col = (g * (GQA_GROUP // 2) + pp) * LANES
            o_ref[n * w:(n + 1) * w, col:col + LANES] = pair_t.T.astype(BF16)

    units = [(n, g) for n in range(nbq) for g in range(N_KV_SWA)]
    cur = None
    for unit in units + [None]:
        nxt = (unit, scores(*unit)) if unit is not None else None
        if cur is not None:
            finish(*cur[0], *cur[1])
        cur = nxt


def _swa(proj, bias_swa, sinks, l):
    bsz, seq, _ = proj.shape
    nbq = seq // SWA_WINDOW
    kern = functools.partial(_swa_kernel, nbq=nbq, layer=l)
    q_col = (3 * W_A) // W_QB
    k_col = (3 * W_A + W_QB) // W_KVB
    return pl.pallas_call(
        kern,
        grid=(bsz,),
        in_specs=[
            pl.BlockSpec(memory_space=pltpu.SMEM),
            pl.BlockSpec((None, seq, W_QB), lambda b: (b, 0, q_col)),
            pl.BlockSpec((None, seq, W_KVB), lambda b: (b, 0, k_col)),
            pl.BlockSpec((None, seq, W_KVB), lambda b: (b, 0, k_col + 1)),
            pl.BlockSpec((N_KV_SWA, 2 * SWA_WINDOW, GQA_GROUP * SWA_WINDOW), lambda b: (0, 0, 0)),
        ],
        out_specs=pl.BlockSpec((None, seq, W_QB), lambda b: (b, 0, 0)),
        out_shape=jax.ShapeDtypeStruct((bsz, seq, W_QB), BF16),
        scratch_shapes=[
            pltpu.VMEM((N_KV_SWA, nbq, LANES, GQA_GROUP * SWA_WINDOW), BF16),
            pltpu.VMEM((N_KV_SWA, seq, LANES), BF16),
            pltpu.VMEM((N_KV_SWA, VT_ROWS, seq), BF16),
        ],
        compiler_params=pltpu.CompilerParams(
            dimension_semantics=("arbitrary",), vmem_limit_bytes=VMEM_LIMIT),
        name="swa",
    )(sinks, proj, proj, proj, bias_swa)


def _mix_mlp_kernel(*refs, n_cast):
    a_ref, b_ref, x_ref, mod_ref, g_ref, wo_ref, w1_ref, b1_ref, w2_ref, b2_ref = refs[:10]
    src_refs = refs[10:10 + n_cast]
    o_ref = refs[10 + n_cast]
    dst_refs = refs[11 + n_cast:]

    y = (jnp.dot(a_ref[...], wo_ref[0:W_A, :], preferred_element_type=F32)
         + jnp.dot(b_ref[...], wo_ref[W_A:, :], preferred_element_type=F32))
    x = x_ref[...] + _rms(y, mod_ref[2:3, :] * g_ref[1:2, :])
    h = _rms(x, g_ref[2:3, :] * (1.0 + mod_ref[4:5, :])) + mod_ref[3:4, :]
    hb = h.astype(BF16)
    acc = jnp.zeros(x.shape, F32)
    for c0 in range(0, D_FF, FF_CHUNK):
        u = jnp.dot(hb, w1_ref[:, c0:c0 + FF_CHUNK], preferred_element_type=F32) + b1_ref[:, c0:c0 + FF_CHUNK]
        u = jnp.square(jnp.maximum(u, 0.0))
        acc = acc + jnp.dot(u.astype(BF16), w2_ref[c0:c0 + FF_CHUNK, :], preferred_element_type=F32)
    y2 = acc + b2_ref[...]
    o_ref[...] = x + _rms(y2, mod_ref[5:6, :] * g_ref[3:4, :])
    for src_ref, dst_ref in zip(src_refs, dst_refs):
        dst_ref[...] = src_ref[...].astype(BF16)


def _mix_mlp(attn_a, attn_b, x, mod, gains, w_out, w1, b1, w2, b2, l, next_f32=()):
    bsz, seq, d = x.shape
    tps = seq // ROW_TILE
    const = lambda shape: pl.BlockSpec(shape, lambda b, t: (0,) * len(shape), pipeline_mode=pl.Buffered(1))
    layer_const = lambda shape: pl.BlockSpec((None,) + shape, lambda b, t: (l,) + (0,) * len(shape))
    cast_in, cast_out, cast_shape = _cast_chunk_specs(next_f32, l + 1, bsz, tps)
    outs = pl.pallas_call(
        functools.partial(_mix_mlp_kernel, n_cast=len(next_f32)),
        grid=(bsz, tps),
        in_specs=[
            pl.BlockSpec((None, ROW_TILE, W_A), lambda b, t: (b, t, 0)),
            pl.BlockSpec((None, ROW_TILE, W_QB), lambda b, t: (b, t, 0)),
            pl.BlockSpec((None, ROW_TILE, d), lambda b, t: (b, t, 0)),
            pl.BlockSpec((None, None, 6, d), lambda b, t: (l, b, 0, 0)),
            layer_const((4, d)),
            const((d, d)),
            const((d, D_FF)),
            layer_const((1, D_FF)),
            const((D_FF, d)),
            layer_const((1, d)),
        ] + cast_in,
        out_specs=[pl.BlockSpec((None, ROW_TILE, d), lambda b, t: (b, t, 0))] + cast_out,
        out_shape=[jax.ShapeDtypeStruct((bsz, seq, d), F32)] + cast_shape,
        compiler_params=pltpu.CompilerParams(
            dimension_semantics=("arbitrary", "arbitrary"), vmem_limit_bytes=VMEM_LIMIT),
        name="mix_mlp",
    )(attn_a, attn_b, x, mod, gains, w_out, w1, b1, w2, b2, *next_f32)
    return outs[0], tuple(outs[1:])


def kernel(x, c, w_in, w_out, sinks, rel_bias, w_ada, b_ada, norm_gains, w1, b1, w2, b2):
    bsz, seq, d = x.shape
    depth = w_in.shape[0]
    assert d == D_MODEL and seq % ROW_TILE == 0 and ROW_TILE % MOBA_BLOCK == 0
    assert seq // MOBA_BLOCK == 8, "the block-gate matmul packs 8 block means per bf16 tile row group"

    mod = _adaln(c, w_ada, b_ada).reshape(depth, bsz, 6, d)
    bias_moba, bias_swa = _bias_tiles(rel_bias)
    cfar = rel_bias[N_BUCKETS - 1, :N_HEADS_MOBA]

    w_in_b = w_in[0].astype(BF16)
    b1_r = b1.reshape(depth, 1, D_FF)
    b2_r = b2.reshape(depth, 1, d)

    for l in range(depth):
        proj, kmean, cast = _inproj(x, mod, norm_gains, w_in_b, l, cast_f32=(w_out, w1, w2) if l == 0 else ())
        if l == 0:
            w_out_b, w1_b, w2_b = cast
        attn_a = _moba(proj, kmean, bias_moba, cfar)
        attn_b = _swa(proj, bias_swa, sinks, l)
        x, cast = _mix_mlp(attn_a, attn_b, x, mod, norm_gains, w_out_b, w1_b, b1_r, w2_b, b2_r, l,
                           next_f32=(w_in, w_out, w1, w2) if l + 1 < depth else ())
        if l + 1 < depth:
            w_in_b, w_out_b, w1_b, w2_b = cast
    return x
```
